```python
import jax, jax.numpy as jnp
from jax import lax
import numpy as np

D_MODEL = 1024
BATCH = 32
SEQ = 2048
DEPTH = 2

D_MIX = D_MODEL
W_A = D_MIX // 4
W_B = D_MIX // 4
W_C = D_MIX // 4
W_D = D_MIX // 4
HEAD_DIM = 64
HEADS_C = W_C // HEAD_DIM
HEADS_D = W_D // HEAD_DIM
K_SHORT = 3
K_CONFORMER = 31
CHUNK = 128
Q_BLOCK = 128
D_FF = 2816
K_FFN = 3
RMS_EPS = 1e-6
LN_EPS = 1e-5
IN_A = 3 * W_A
IN_B = 2 * W_B
IN_C = 2 * W_C
IN_D = 3 * W_D
IN_TOTAL = IN_A + IN_B + IN_C + IN_D

kernel_name = "hybrid_parallel_conv_sgu_stickbreaking"


def _rmsnorm(x, g):
    xf = x.astype(jnp.float32)
    y = xf * lax.rsqrt(jnp.mean(xf * xf, axis=-1, keepdims=True) + RMS_EPS)
    return (y * g.astype(jnp.float32)).astype(x.dtype)


def _layernorm(x, g, b):
    xf = x.astype(jnp.float32)
    mu = jnp.mean(xf, axis=-1, keepdims=True)
    xc = xf - mu
    var = jnp.mean(xc * xc, axis=-1, keepdims=True)
    y = xc * lax.rsqrt(var + LN_EPS) * g.astype(jnp.float32) + b.astype(jnp.float32)
    return y.astype(x.dtype)


def _causal_dwconv(x, w):
    k = w.shape[0]
    return lax.conv_general_dilated(
        x, w[:, None, :].astype(x.dtype), window_strides=(1,), padding=[(k - 1, 0)],
        dimension_numbers=("NWC", "WIO", "NWC"), feature_group_count=x.shape[-1])


def _stick_breaking(q, k, v):
    s_len = q.shape[2]
    scale = q.shape[-1] ** -0.5
    outs = []
    for i in range(s_len // Q_BLOCK):
        q0 = i * Q_BLOCK
        kv_len = q0 + Q_BLOCK
        qb = q[:, :, q0:kv_len]
        kb = k[:, :, :kv_len]
        vb = v[:, :, :kv_len]
        z = jnp.einsum("bhtd,bhsd->bhts", qb, kb).astype(jnp.float32) * scale
        t_idx = q0 + jnp.arange(Q_BLOCK)[:, None]
        s_idx = jnp.arange(kv_len)[None, :]
        mask = s_idx < t_idx
        log_beta = jax.nn.log_sigmoid(z)
        log_one_minus = jnp.where(mask, log_beta - z, 0.0)
        after = lax.cumsum(log_one_minus, axis=3, reverse=True) - log_one_minus
        a = jnp.exp(jnp.where(mask, log_beta + after, -jnp.inf))
        outs.append(jnp.einsum("bhts,bhsd->bhtd", a.astype(vb.dtype), vb))
    return jnp.concatenate(outs, axis=2)


def setup_inputs(seed: int = 0) -> dict:
    key = jax.random.key(seed)
    ks = jax.random.split(key, 20)
    f32 = jnp.float32
    nrm = lambda k, shape, s: jax.random.normal(k, shape, f32) * s
    x = jax.random.normal(ks[0], (BATCH, SEQ, D_MODEL), f32)
    norm1_g = 1.0 + nrm(ks[1], (DEPTH, D_MODEL), 0.05)
    w_in = nrm(ks[2], (DEPTH, D_MODEL, IN_TOTAL), D_MODEL ** -0.5)
    conv_a_w = nrm(ks[3], (DEPTH, K_SHORT, W_A), K_SHORT ** -0.5)
    conv_b_w = nrm(ks[4], (DEPTH, K_CONFORMER, W_B), K_CONFORMER ** -0.5)
    conv_b_b = nrm(ks[5], (DEPTH, W_B), 0.02)
    ln_b_g = 1.0 + nrm(ks[6], (DEPTH, W_B), 0.05)
    ln_b_b = nrm(ks[7], (DEPTH, W_B), 0.02)
    ln_c_g = 1.0 + nrm(ks[8], (DEPTH, W_C), 0.05)
    ln_c_b = nrm(ks[9], (DEPTH, W_C), 0.02)
    sgu_w = nrm(ks[10], (DEPTH, HEADS_C, CHUNK, CHUNK), 0.5 * CHUNK ** -0.5)
    sgu_b = 1.0 + nrm(ks[11], (DEPTH, HEADS_C, CHUNK), 0.1)
    w_out = nrm(ks[12], (DEPTH, D_MIX, D_MODEL), D_MIX ** -0.5)
    norm2_g = 1.0 + nrm(ks[13], (DEPTH, D_MODEL), 0.05)
    w_up = nrm(ks[14], (DEPTH, D_MODEL, 2 * D_FF), D_MODEL ** -0.5)
    conv_f_w = nrm(ks[15], (DEPTH, K_FFN, 2 * D_FF), K_FFN ** -0.5)
    w_down = nrm(ks[16], (DEPTH, D_FF, D_MODEL), D_FF ** -0.5)
    final_g = 1.0 + nrm(ks[17], (D_MODEL,), 0.05)
    return {"x": x, "norm1_g": norm1_g, "w_in": w_in, "conv_a_w": conv_a_w,
            "conv_b_w": conv_b_w, "conv_b_b": conv_b_b, "ln_b_g": ln_b_g, "ln_b_b": ln_b_b,
            "ln_c_g": ln_c_g, "ln_c_b": ln_c_b, "sgu_w": sgu_w, "sgu_b": sgu_b,
            "w_out": w_out, "norm2_g": norm2_g, "w_up": w_up, "conv_f_w": conv_f_w,
            "w_down": w_down, "final_g": final_g}


def reference(x, norm1_g, w_in, conv_a_w, conv_b_w, conv_b_b, ln_b_g, ln_b_b,
              ln_c_g, ln_c_b, sgu_w, sgu_b, w_out, norm2_g, w_up, conv_f_w,
              w_down, final_g):
    bsz, s_len, _ = x.shape
    n_chunks = s_len // CHUNK
    tri = jnp.tril(jnp.ones((CHUNK, CHUNK), dtype=x.dtype))
    for l in range(DEPTH):
        h = _rmsnorm(x, norm1_g[l])
        p = jnp.einsum("bsd,de->bse", h, w_in[l])
        p_a, p_b, p_c, p_d = jnp.split(p, [IN_A, IN_A + IN_B, IN_A + IN_B + IN_C], axis=-1)

        gate_b, gate_c, h_a = jnp.split(p_a, 3, axis=-1)
        y_a = gate_b * _causal_dwconv(gate_c * h_a, conv_a_w[l])

        val_b, gat_b = jnp.split(p_b, 2, axis=-1)
        glu = val_b * jax.nn.sigmoid(gat_b)
        cb = _causal_dwconv(glu, conv_b_w[l]) + conv_b_b[l]
        y_b = jax.nn.silu(_layernorm(cb, ln_b_g[l], ln_b_b[l]))

        uv = jax.nn.gelu(p_c, approximate=False)
        u, v_c = jnp.split(uv, 2, axis=-1)
        v_c = _layernorm(v_c, ln_c_g[l], ln_c_b[l])
        v_c = v_c.reshape(bsz, n_chunks, CHUNK, HEADS_C, HEAD_DIM)
        ws = sgu_w[l] * tri
        sp = jnp.einsum("gts,bnsgc->bntgc", ws, v_c) + sgu_b[l].T[None, None, :, :, None]
        y_c = u * sp.reshape(bsz, s_len, W_C)

        q, k, v = jnp.split(p_d, 3, axis=-1)
        to_heads = lambda t: t.reshape(bsz, s_len, HEADS_D, HEAD_DIM).transpose(0, 2, 1, 3)
        o_d = _stick_breaking(to_heads(q), to_heads(k), to_heads(v))
        y_d = o_d.transpose(0, 2, 1, 3).reshape(bsz, s_len, W_D)

        mix = jnp.concatenate([y_a, y_b, y_c, y_d], axis=-1)
        x = x + jnp.einsum("bse,ed->bsd", mix, w_out[l])

        h2 = _rmsnorm(x, norm2_g[l])
        up = _causal_dwconv(jnp.einsum("bsd,df->bsf", h2, w_up[l]), conv_f_w[l])
        g_f, v_f = jnp.split(up, 2, axis=-1)
        x = x + jnp.einsum("bsf,fd->bsd", jax.nn.silu(g_f) * v_f, w_down[l])
    return _rmsnorm(x, final_g)
```

```python
import functools

import jax
import jax.numpy as jnp
from jax import lax
from jax.experimental import pallas as pl
from jax.experimental.pallas import tpu as pltpu

F32 = jnp.float32
BF16 = jnp.bfloat16

HEAD_DIM = 64
N_HEADS = 4
W_MIX = N_HEADS * HEAD_DIM
BLK = 128
K_SHORT = 3
K_CONF = 31
K_FFN = 3
RMS_EPS = 1e-6
LN_EPS = 1e-5
SUBLANES = 8
A_HDR = SUBLANES
B_HDR = 4 * SUBLANES
CONV_ROWS = 64
VMEM_LIMIT_BYTES = 56 * 1024 * 1024


def _dot(a, b):
    return jnp.dot(a, b, preferred_element_type=F32)


def _dot_nt(a, b):
    return lax.dot_general(a, b, (((1,), (1,)), ((), ())), preferred_element_type=F32)


def _rmsnorm(x, g):
    return x * lax.rsqrt(jnp.mean(x * x, axis=-1, keepdims=True) + RMS_EPS) * g


def _layernorm(x, g, b):
    mu = jnp.mean(x, axis=-1, keepdims=True)
    xc = x - mu
    var = jnp.mean(xc * xc, axis=-1, keepdims=True)
    return xc * lax.rsqrt(var + LN_EPS) * g + b


def _silu(x):
    return x * jax.nn.sigmoid(x)


def _split_bf16(x):
    hi = x.astype(BF16)
    lo = (x - hi.astype(F32)).astype(BF16)
    return hi, lo


def _mixer_kernel(x_ref, g1_ref, win_ref, caw_ref, cbw_ref, cbb_ref, lnbg_ref, lnbb_ref,
                  lncg_ref, lncb_ref, sguw_ref, sgub_ref, wout_ref, o_ref,
                  kbd_ref, vbd_ref, abuf, bbuf, q_ref, mix_ref, *, ts):
    j = pl.program_id(1)
    nblk = ts // BLK

    @pl.when(j == 0)
    def _():
        abuf[0:A_HDR, :] = jnp.zeros((A_HDR, W_MIX), F32)
        bbuf[0:B_HDR, :] = jnp.zeros((B_HDR, W_MIX), F32)

    x = x_ref[0]
    h = _rmsnorm(x, g1_ref[...]).astype(BF16)

    pa = _dot(h, win_ref[:, 0:3 * W_MIX])
    abuf[A_HDR:A_HDR + ts, :] = pa[:, W_MIX:2 * W_MIX] * pa[:, 2 * W_MIX:3 * W_MIX]
    conv_a = jnp.zeros((ts, W_MIX), F32)
    for k in range(K_SHORT):
        off = A_HDR - (K_SHORT - 1) + k
        conv_a = conv_a + caw_ref[k:k + 1, :] * abuf[off:off + ts, :]
    mix_ref[:, 0:W_MIX] = (pa[:, 0:W_MIX] * conv_a).astype(BF16)
    abuf[0:A_HDR, :] = abuf[ts:ts + A_HDR, :]

    pb = _dot(h, win_ref[:, 3 * W_MIX:5 * W_MIX])
    bbuf[B_HDR:B_HDR + ts, :] = pb[:, 0:W_MIX] * jax.nn.sigmoid(pb[:, W_MIX:2 * W_MIX])
    for r in range(ts // CONV_ROWS):
        acc = jnp.zeros((CONV_ROWS, W_MIX), F32) + cbb_ref[...]
        for k in range(K_CONF):
            off = r * CONV_ROWS + B_HDR - (K_CONF - 1) + k
            acc = acc + cbw_ref[k:k + 1, :] * bbuf[off:off + CONV_ROWS, :]
        yb = _silu(_layernorm(acc, lnbg_ref[...], lnbb_ref[...]))
        mix_ref[r * CONV_ROWS:(r + 1) * CONV_ROWS, W_MIX:2 * W_MIX] = yb.astype(BF16)
    bbuf[0:B_HDR, :] = bbuf[ts:ts + B_HDR, :]

    lane_head = lax.broadcasted_iota(jnp.int32, (BLK, W_MIX), 1) // HEAD_DIM

    def head_stack(blk):
        parts = [jnp.where(lane_head == g, blk, 0.0) for g in range(N_HEADS)]
        return jnp.concatenate(parts, axis=0).astype(BF16)

    pc = _dot(h, win_ref[:, 5 * W_MIX:7 * W_MIX])
    uv = 0.5 * pc * (1.0 + lax.erf(pc * (2.0 ** -0.5)))
    u_c = uv[:, 0:W_MIX]
    v_c = _layernorm(uv[:, W_MIX:2 * W_MIX], lncg_ref[...], lncb_ref[...])
    row = lax.broadcasted_iota(jnp.int32, (BLK, BLK), 0)
    col = lax.broadcasted_iota(jnp.int32, (BLK, BLK), 1)
    ws_cat = jnp.concatenate(
        [jnp.where(col <= row, sguw_ref[g], 0.0) for g in range(N_HEADS)], axis=1).astype(BF16)
    for n in range(nblk):
        rows = slice(n * BLK, (n + 1) * BLK)
        sp = _dot(ws_cat, head_stack(v_c[rows, :])) + sgub_ref[...]
        mix_ref[rows, 2 * W_MIX:3 * W_MIX] = (u_c[rows, :] * sp).astype(BF16)

    pd = _dot(h, win_ref[:, 7 * W_MIX:10 * W_MIX])
    q_ref[...] = (pd[:, 0:W_MIX] * (HEAD_DIM ** -0.5)).astype(BF16)
    for n in range(nblk):
        rows = slice(n * BLK, (n + 1) * BLK)
        kbd_ref[j * nblk + n] = head_stack(pd[rows, W_MIX:2 * W_MIX])
        vbd_ref[j * nblk + n] = head_stack(pd[rows, 2 * W_MIX:3 * W_MIX])

    r2 = lax.broadcasted_iota(jnp.int32, (2 * BLK, 4 * BLK), 0)
    c2 = lax.broadcasted_iota(jnp.int32, (2 * BLK, 4 * BLK), 1)
    same_head = (r2 // BLK) == ((c2 // BLK) % 2)
    strict = (r2 % BLK) > (c2 % BLK)
    ubd = jnp.where(same_head & (strict | (c2 >= 2 * BLK)), 1.0, 0.0).astype(BF16)

    t_loc = lax.broadcasted_iota(jnp.int32, (BLK, N_HEADS * BLK), 0)
    s_loc = lax.broadcasted_iota(jnp.int32, (BLK, N_HEADS * BLK), 1) % BLK
    causal = s_loc < t_loc

    def attend(q, kblk, vblk, carry, diag):
        z = _dot_nt(q, kblk)
        sp = jnp.log(1.0 + jnp.exp(-jnp.abs(z)))
        log_beta = jnp.minimum(z, 0.0) - sp
        lom = log_beta - z
        if diag:
            lom = jnp.where(causal, lom, 0.0)
        hi, lo = _split_bf16(lom)
        after, tot = [], []
        for p in range(N_HEADS // 2):
            cols = slice(p * 2 * BLK, (p + 1) * 2 * BLK)
            cs = _dot(hi[:, cols], ubd) + _dot(lo[:, cols], ubd)
            after.append(cs[:, 0:2 * BLK])
            tot.append(cs[:, 2 * BLK:4 * BLK])
        after = jnp.concatenate(after, axis=1)
        tot = jnp.concatenate(tot, axis=1)
        logit = log_beta + after + carry
        if diag:
            logit = jnp.where(causal, logit, -jnp.inf)
        a = jnp.exp(logit).astype(BF16)
        return _dot(a, vblk), carry + tot

    for n in range(nblk):
        rows = slice(n * BLK, (n + 1) * BLK)
        q = q_ref[rows, :]
        qb = j * nblk + n
        o, carry = attend(q, kbd_ref[qb], vbd_ref[qb],
                          jnp.zeros((BLK, N_HEADS * BLK), F32), True)

        def body(i, oc):
            o_acc, c_acc = oc
            kb = qb - 1 - i
            do, c_new = attend(q, kbd_ref[kb], vbd_ref[kb], c_acc, False)
            return o_acc + do, c_new

        o, _ = lax.fori_loop(0, qb, body, (o, carry))
        mix_ref[rows, 3 * W_MIX:4 * W_MIX] = o.astype(BF16)

    o_ref[0] = x + _dot(mix_ref[...], wout_ref[...])


def _ffn_kernel(x_ref, g2_ref, wup_ref, cfw_ref, wdn_ref, gf_ref, o_ref, gbuf, vbuf,
                *, tm, d_ff, fc, final_norm):
    j = pl.program_id(1)
    nc = d_ff // fc
    hdr = SUBLANES

    @pl.when(j == 0)
    def _():
        gbuf[:, 0:hdr, :] = jnp.zeros((nc, hdr, fc), F32)
        vbuf[:, 0:hdr, :] = jnp.zeros((nc, hdr, fc), F32)

    x = x_ref[0]
    h2 = _rmsnorm(x, g2_ref[...]).astype(BF16)
    acc = x
    for c in range(nc):
        halves = []
        for buf, base in ((gbuf, 0), (vbuf, d_ff)):
            cols = slice(base + c * fc, base + (c + 1) * fc)
            buf[c, hdr:hdr + tm, :] = _dot(h2, wup_ref[:, cols])
            conv = jnp.zeros((tm, fc), F32)
            for k in range(K_FFN):
                off = hdr - (K_FFN - 1) + k
                conv = conv + cfw_ref[k:k + 1, cols] * buf[c, off:off + tm, :]
            buf[c, 0:hdr, :] = buf[c, tm:tm + hdr, :]
            halves.append(conv)
        act = (_silu(halves[0]) * halves[1]).astype(BF16)
        acc = acc + _dot(act, wdn_ref[c * fc:(c + 1) * fc, :])
    if final_norm:
        acc = _rmsnorm(acc, gf_ref[...])
    o_ref[0] = acc


def _full(shape):
    return pl.BlockSpec(shape, lambda b, j: (0,) * len(shape), pipeline_mode=pl.Buffered(1))


def _mixer_call(x, g1, w_in, caw, cbw, cbb, lnbg, lnbb, lncg, lncb, sguw, sgub, w_out, *, ts):
    bsz, s_len, d = x.shape
    tile = pl.BlockSpec((1, ts, d), lambda b, j: (b, j, 0))
    args = (x, g1, w_in, caw, cbw, cbb, lnbg, lnbb, lncg, lncb, sguw, sgub, w_out)
    in_specs = [tile] + [_full(a.shape) for a in args[1:]]
    return pl.pallas_call(
        functools.partial(_mixer_kernel, ts=ts),
        grid=(bsz, s_len // ts),
        in_specs=in_specs,
        out_specs=tile,
        out_shape=jax.ShapeDtypeStruct(x.shape, x.dtype),
        scratch_shapes=[
            pltpu.VMEM((s_len // BLK, N_HEADS * BLK, W_MIX), BF16),
            pltpu.VMEM((s_len // BLK, N_HEADS * BLK, W_MIX), BF16),
            pltpu.VMEM((A_HDR + ts, W_MIX), F32),
            pltpu.VMEM((B_HDR + ts, W_MIX), F32),
            pltpu.VMEM((ts, W_MIX), BF16),
            pltpu.VMEM((ts, d), BF16),
        ],
        compiler_params=pltpu.CompilerParams(
            dimension_semantics=("arbitrary", "arbitrary"),
            vmem_limit_bytes=VMEM_LIMIT_BYTES),
        name="mixer",
    )(*args)


def _ffn_call(x, g2, w_up, cfw, w_dn, gf, *, tm, fc, final_norm):
    bsz, s_len, d = x.shape
    d_ff = w_dn.shape[0]
    tile = pl.BlockSpec((1, tm, d), lambda b, j: (b, j, 0))
    args = (x, g2, w_up, cfw, w_dn, gf)
    in_specs = [tile] + [_full(a.shape) for a in args[1:]]
    return pl.pallas_call(
        functools.partial(_ffn_kernel, tm=tm, d_ff=d_ff, fc=fc, final_norm=final_norm),
        grid=(bsz, s_len // tm),
        in_specs=in_specs,
        out_specs=tile,
        out_shape=jax.ShapeDtypeStruct(x.shape, x.dtype),
        scratch_shapes=[
            pltpu.VMEM((d_ff // fc, SUBLANES + tm, fc), F32),
            pltpu.VMEM((d_ff // fc, SUBLANES + tm, fc), F32),
        ],
        compiler_params=pltpu.CompilerParams(
            dimension_semantics=("arbitrary", "arbitrary"),
            vmem_limit_bytes=VMEM_LIMIT_BYTES),
        name="ffn",
    )(*args)


def kernel(x, norm1_g, w_in, conv_a_w, conv_b_w, conv_b_b, ln_b_g, ln_b_b, ln_c_g, ln_c_b,
           sgu_w, sgu_b, w_out, norm2_g, w_up, conv_f_w, w_down, final_g):
    depth = w_in.shape[0]
    row = lambda v: v[None, :]
    for l in range(depth):
        sgub = jnp.repeat(sgu_b[l].T, HEAD_DIM, axis=1)
        x = _mixer_call(
            x, row(norm1_g[l]), w_in[l].astype(BF16), conv_a_w[l], conv_b_w[l], row(conv_b_b[l]),
            row(ln_b_g[l]), row(ln_b_b[l]), row(ln_c_g[l]), row(ln_c_b[l]), sgu_w[l], sgub,
            w_out[l].astype(BF16), ts=512)
        x = _ffn_call(
            x, row(norm2_g[l]), w_up[l].astype(BF16), conv_f_w[l], w_down[l].astype(BF16),
            row(final_g), tm=512, fc=256, final_norm=(l == depth - 1))
    return x
```

```python
import functools

import jax
import jax.numpy as jnp
from jax import lax
from jax.experimental import pallas as pl
from jax.experimental.pallas import tpu as pltpu

F32 = jnp.float32
BF16 = jnp.bfloat16

HEAD_DIM = 64
N_HEADS = 4
W_MIX = N_HEADS * HEAD_DIM
BLK = 128
K_SHORT = 3
K_CONF = 31
K_FFN = 3
RMS_EPS = 1e-6
LN_EPS = 1e-5
SUBLANES = 8
A_HDR = SUBLANES
B_HDR = 4 * SUBLANES
CONV_ROWS = 64
VMEM_LIMIT_BYTES = 56 * 1024 * 1024


def _dot(a, b):
    return jnp.dot(a, b, preferred_element_type=F32)


def _dot_nt(a, b):
    return lax.dot_general(a, b, (((1,), (1,)), ((), ())), preferred_element_type=F32)


def _rmsnorm(x, g):
    return x * lax.rsqrt(jnp.mean(x * x, axis=-1, keepdims=True) + RMS_EPS) * g


def _layernorm(x, g, b):
    mu = jnp.mean(x, axis=-1, keepdims=True)
    xc = x - mu
    var = jnp.mean(xc * xc, axis=-1, keepdims=True)
    return xc * lax.rsqrt(var + LN_EPS) * g + b


def _silu(x):
    return x * jax.nn.sigmoid(x)


def _split_bf16(x):
    hi = x.astype(BF16)
    lo = (x - hi.astype(F32)).astype(BF16)
    return hi, lo


def _mixer_kernel(x_ref, g1_ref, win_ref, caw_ref, cbw_ref, cbb_ref, lnbg_ref, lnbb_ref,
                  lncg_ref, lncb_ref, sguw_ref, sgub_ref, wout_ref, o_ref,
                  kbd_ref, vbd_ref, abuf, bbuf, sbuf, q_ref, oacc, crun, mix_ref, *, ts):
    j = pl.program_id(1)
    nblk = ts // BLK

    @pl.when(j == 0)
    def _():
        abuf[0:A_HDR, :] = jnp.zeros((A_HDR, W_MIX), F32)
        bbuf[0:B_HDR, :] = jnp.zeros((B_HDR, W_MIX), F32)

    x = x_ref[0]
    h = _rmsnorm(x, g1_ref[...]).astype(BF16)

    pa = _dot(h, win_ref[:, 0:3 * W_MIX])
    abuf[A_HDR:A_HDR + ts, :] = pa[:, W_MIX:2 * W_MIX] * pa[:, 2 * W_MIX:3 * W_MIX]
    conv_a = jnp.zeros((ts, W_MIX), F32)
    for k in range(K_SHORT):
        off = A_HDR - (K_SHORT - 1) + k
        conv_a = conv_a + caw_ref[k:k + 1, :] * abuf[off:off + ts, :]
    mix_ref[:, 0:W_MIX] = (pa[:, 0:W_MIX] * conv_a).astype(BF16)
    abuf[0:A_HDR, :] = abuf[ts:ts + A_HDR, :]

    pb = _dot(h, win_ref[:, 3 * W_MIX:5 * W_MIX])
    bbuf[B_HDR:B_HDR + ts, :] = pb[:, 0:W_MIX] * jax.nn.sigmoid(pb[:, W_MIX:2 * W_MIX])
    n_shift_rows = ts + B_HDR - SUBLANES
    for s in range(1, SUBLANES):
        sbuf[s - 1, :, :] = bbuf[s:s + n_shift_rows, :]
    for r in range(ts // CONV_ROWS):
        acc = jnp.zeros((CONV_ROWS, W_MIX), F32) + cbb_ref[...]
        for k in range(K_CONF):
            grp, s = divmod(B_HDR - (K_CONF - 1) + k, SUBLANES)
            off = r * CONV_ROWS + grp * SUBLANES
            src = bbuf[off:off + CONV_ROWS, :] if s == 0 else sbuf[s - 1, off:off + CONV_ROWS, :]
            acc = acc + cbw_ref[k:k + 1, :] * src
        yb = _silu(_layernorm(acc, lnbg_ref[...], lnbb_ref[...]))
        mix_ref[r * CONV_ROWS:(r + 1) * CONV_ROWS, W_MIX:2 * W_MIX] = yb.astype(BF16)
    bbuf[0:B_HDR, :] = bbuf[ts:ts + B_HDR, :]

    lane_head = lax.broadcasted_iota(jnp.int32, (BLK, W_MIX), 1) // HEAD_DIM

    def head_stack(blk):
        parts = [jnp.where(lane_head == g, blk, 0.0) for g in range(N_HEADS)]
        return jnp.concatenate(parts, axis=0).astype(BF16)

    pc = _dot(h, win_ref[:, 5 * W_MIX:7 * W_MIX])
    uv = 0.5 * pc * (1.0 + lax.erf(pc * (2.0 ** -0.5)))
    u_c = uv[:, 0:W_MIX]
    v_c = _layernorm(uv[:, W_MIX:2 * W_MIX], lncg_ref[...], lncb_ref[...])
    row = lax.broadcasted_iota(jnp.int32, (BLK, BLK), 0)
    col = lax.broadcasted_iota(jnp.int32, (BLK, BLK), 1)
    ws_cat = jnp.concatenate(
        [jnp.where(col <= row, sguw_ref[g], 0.0) for g in range(N_HEADS)], axis=1).astype(BF16)
    for n in range(nblk):
        rows = slice(n * BLK, (n + 1) * BLK)
        sp = _dot(ws_cat, head_stack(v_c[rows, :])) + sgub_ref[...]
        mix_ref[rows, 2 * W_MIX:3 * W_MIX] = (u_c[rows, :] * sp).astype(BF16)

    pd = _dot(h, win_ref[:, 7 * W_MIX:10 * W_MIX])
    q_ref[...] = (pd[:, 0:W_MIX] * (HEAD_DIM ** -0.5)).astype(BF16)
    for n in range(nblk):
        rows = slice(n * BLK, (n + 1) * BLK)
        kbd_ref[j * nblk + n] = head_stack(pd[rows, W_MIX:2 * W_MIX])
        vbd_ref[j * nblk + n] = head_stack(pd[rows, 2 * W_MIX:3 * W_MIX])

    r2 = lax.broadcasted_iota(jnp.int32, (2 * BLK, 2 * BLK), 0)
    c2 = lax.broadcasted_iota(jnp.int32, (2 * BLK, 2 * BLK), 1)
    ubd = jnp.where((r2 > c2) & ((r2 < BLK) | (c2 >= BLK)), 1.0, 0.0).astype(BF16)

    t_loc = lax.broadcasted_iota(jnp.int32, (BLK, N_HEADS * BLK), 0)
    s_loc = lax.broadcasted_iota(jnp.int32, (BLK, N_HEADS * BLK), 1) % BLK
    causal = s_loc < t_loc

    def mask_own_block(v, fill):
        own = jnp.where(causal, v[0:BLK], fill)
        return own if v.shape[0] == BLK else jnp.concatenate([own, v[BLK:]], axis=0)

    def attend(q, kblk, vblk, carry, diag):
        z = _dot_nt(q, kblk)
        sp = jnp.log(1.0 + jnp.exp(-jnp.abs(z)))
        log_beta = jnp.minimum(z, 0.0) - sp
        lom = log_beta - z
        if diag:
            lom = mask_own_block(lom, 0.0)
        hi, lo = _split_bf16(lom)
        after = []
        for p in range(N_HEADS // 2):
            cols = slice(p * 2 * BLK, (p + 1) * 2 * BLK)
            after.append(_dot(hi[:, cols], ubd) + _dot(lo[:, cols], ubd))
        tot = []
        for g in range(N_HEADS):
            row_sum = jnp.sum(lom[:, g * BLK:(g + 1) * BLK], axis=-1, keepdims=True)
            tot.append(jnp.broadcast_to(row_sum, (lom.shape[0], BLK)))
        logit = log_beta + jnp.concatenate(after, axis=1) + carry
        if diag:
            logit = mask_own_block(logit, -jnp.inf)
        a = jnp.exp(logit).astype(BF16)
        return _dot(a, vblk), jnp.concatenate(tot, axis=1)

    oacc[...] = jnp.zeros((ts, W_MIX), F32)
    crun[...] = jnp.zeros((ts, N_HEADS * BLK), F32)
    for m in reversed(range(nblk)):
        rows = slice(m * BLK, ts)
        do, tot = attend(q_ref[rows, :], kbd_ref[j * nblk + m], vbd_ref[j * nblk + m],
                         crun[rows, :], True)
        oacc[rows, :] += do
        crun[rows, :] += tot

    def older_block(i, _):
        kb = j * nblk - 1 - i
        do, tot = attend(q_ref[...], kbd_ref[kb], vbd_ref[kb], crun[...], False)
        oacc[...] += do
        crun[...] += tot
        return 0

    lax.fori_loop(0, j * nblk, older_block, 0)
    mix_ref[:, 3 * W_MIX:4 * W_MIX] = oacc[...].astype(BF16)

    o_ref[0] = x + _dot(mix_ref[...], wout_ref[...])


def _ffn_kernel(x_ref, g2_ref, wup_ref, cfw_ref, wdn_ref, gf_ref, o_ref, gbuf, vbuf,
                *, tm, d_ff, fc, final_norm):
    j = pl.program_id(1)
    nc = d_ff // fc
    hdr = SUBLANES

    @pl.when(j == 0)
    def _():
        gbuf[:, 0:hdr, :] = jnp.zeros((nc, hdr, fc), F32)
        vbuf[:, 0:hdr, :] = jnp.zeros((nc, hdr, fc), F32)

    x = x_ref[0]
    h2 = _rmsnorm(x, g2_ref[...]).astype(BF16)
    acc = x

    def up_proj(c):
        for buf, base in ((gbuf, 0), (vbuf, d_ff)):
            cols = slice(base + c * fc, base + (c + 1) * fc)
            buf[c, hdr:hdr + tm, :] = _dot(h2, wup_ref[:, cols])

    def conv_gate(c):
        halves = []
        for buf, base in ((gbuf, 0), (vbuf, d_ff)):
            cols = slice(base + c * fc, base + (c + 1) * fc)
            conv = jnp.zeros((tm, fc), F32)
            for k in range(K_FFN):
                off = hdr - (K_FFN - 1) + k
                conv = conv + cfw_ref[k:k + 1, cols] * buf[c, off:off + tm, :]
            buf[c, 0:hdr, :] = buf[c, tm:tm + hdr, :]
            halves.append(conv)
        return (_silu(halves[0]) * halves[1]).astype(BF16)

    acts = {}
    for i in range(nc + 2):
        if i < nc:
            up_proj(i)
        if 0 <= i - 1 < nc:
            acts[i - 1] = conv_gate(i - 1)
        if 0 <= i - 2 < nc:
            acc = acc + _dot(acts.pop(i - 2), wdn_ref[(i - 2) * fc:(i - 1) * fc, :])
    if final_norm:
        acc = _rmsnorm(acc, gf_ref[...])
    o_ref[0] = acc


def _full(shape):
    return pl.BlockSpec(shape, lambda b, j: (0,) * len(shape), pipeline_mode=pl.Buffered(1))


def _mixer_call(x, g1, w_in, caw, cbw, cbb, lnbg, lnbb, lncg, lncb, sguw, sgub, w_out, *, ts):
    bsz, s_len, d = x.shape
    tile = pl.BlockSpec((1, ts, d), lambda b, j: (b, j, 0))
    args = (x, g1, w_in, caw, cbw, cbb, lnbg, lnbb, lncg, lncb, sguw, sgub, w_out)
    in_specs = [tile] + [_full(a.shape) for a in args[1:]]
    return pl.pallas_call(
        functools.partial(_mixer_kernel, ts=ts),
        grid=(bsz, s_len // ts),
        in_specs=in_specs,
        out_specs=tile,
        out_shape=jax.ShapeDtypeStruct(x.shape, x.dtype),
        scratch_shapes=[
            pltpu.VMEM((s_len // BLK, N_HEADS * BLK, W_MIX), BF16),
            pltpu.VMEM((s_len // BLK, N_HEADS * BLK, W_MIX), BF16),
            pltpu.VMEM((A_HDR + ts, W_MIX), F32),
            pltpu.VMEM((B_HDR + ts, W_MIX), F32),
            pltpu.VMEM((SUBLANES - 1, ts + B_HDR - SUBLANES, W_MIX), F32),
            pltpu.VMEM((ts, W_MIX), BF16),
            pltpu.VMEM((ts, W_MIX), F32),
            pltpu.VMEM((ts, N_HEADS * BLK), F32),
            pltpu.VMEM((ts, d), BF16),
        ],
        compiler_params=pltpu.CompilerParams(
            dimension_semantics=("arbitrary", "arbitrary"),
            vmem_limit_bytes=VMEM_LIMIT_BYTES),
        name="mixer",
    )(*args)


def _ffn_call(x, g2, w_up, cfw, w_dn, gf, *, tm, fc, final_norm):
    bsz, s_len, d = x.shape
    d_ff = w_dn.shape[0]
    tile = pl.BlockSpec((1, tm, d), lambda b, j: (b, j, 0))
    args = (x, g2, w_up, cfw, w_dn, gf)
    in_specs = [tile] + [_full(a.shape) for a in args[1:]]
    return pl.pallas_call(
        functools.partial(_ffn_kernel, tm=tm, d_ff=d_ff, fc=fc, final_norm=final_norm),
        grid=(bsz, s_len // tm),
        in_specs=in_specs,
        out_specs=tile,
        out_shape=jax.ShapeDtypeStruct(x.shape, x.dtype),
        scratch_shapes=[
            pltpu.VMEM((d_ff // fc, SUBLANES + tm, fc), F32),
            pltpu.VMEM((d_ff // fc, SUBLANES + tm, fc), F32),
        ],
        compiler_params=pltpu.CompilerParams(
            dimension_semantics=("arbitrary", "arbitrary"),
            vmem_limit_bytes=VMEM_LIMIT_BYTES),
        name="ffn",
    )(*args)


def kernel(x, norm1_g, w_in, conv_a_w, conv_b_w, conv_b_b, ln_b_g, ln_b_b, ln_c_g, ln_c_b,
           sgu_w, sgu_b, w_out, norm2_g, w_up, conv_f_w, w_down, final_g):
    depth = w_in.shape[0]
    row = lambda v: v[None, :]
    for l in range(depth):
        sgub = jnp.repeat(sgu_b[l].T, HEAD_DIM, axis=1)
        x = _mixer_call(
            x, row(norm1_g[l]), w_in[l].astype(BF16), conv_a_w[l], conv_b_w[l], row(conv_b_b[l]),
            row(ln_b_g[l]), row(ln_b_b[l]), row(ln_c_g[l]), row(ln_c_b[l]), sgu_w[l], sgub,
            w_out[l].astype(BF16), ts=512)
        x = _ffn_call(
            x, row(norm2_g[l]), w_up[l].astype(BF16), conv_f_w[l], w_down[l].astype(BF16),
            row(final_g), tm=512, fc=256, final_norm=(l == depth - 1))
    return x
```

```python
import functools

import jax
import jax.numpy as jnp
from jax import lax
from jax.experimental import pallas as pl
from jax.experimental.pallas import tpu as pltpu

F32 = jnp.float32
BF16 = jnp.bfloat16

HEAD_DIM = 64
N_HEADS = 4
W_MIX = N_HEADS * HEAD_DIM
BLK = 128
K_SHORT = 3
K_CONF = 31
K_FFN = 3
RMS_EPS = 1e-6
LN_EPS = 1e-5
SUBLANES = 8
A_HDR = SUBLANES
B_HDR = 4 * SUBLANES
CONV_ROWS = 64
VMEM_LIMIT_BYTES = 56 * 1024 * 1024
EXP_FLOOR = -110.0
OLDER_UNROLL = 2


def _dot(a, b):
    return jnp.dot(a, b, preferred_element_type=F32)


def _dot_nt(a, b):
    return lax.dot_general(a, b, (((1,), (1,)), ((), ())), preferred_element_type=F32)


def _rmsnorm(x, g):
    return x * lax.rsqrt(jnp.mean(x * x, axis=-1, keepdims=True) + RMS_EPS) * g


def _layernorm(x, g, b):
    mu = jnp.mean(x, axis=-1, keepdims=True)
    xc = x - mu
    var = jnp.mean(xc * xc, axis=-1, keepdims=True)
    return xc * lax.rsqrt(var + LN_EPS) * g + b


def _silu(x):
    return x * jax.nn.sigmoid(x)


def _split_bf16(x):
    hi = x.astype(BF16)
    lo = (x - hi.astype(F32)).astype(BF16)
    return hi, lo


def _mixer_kernel(x_ref, g1_ref, win_ref, caw_ref, cbw_ref, cbb_ref, lnbg_ref, lnbb_ref,
                  lncg_ref, lncb_ref, sguw_ref, sgub_ref, wout_ref, o_ref,
                  kbd_ref, vbd_ref, abuf, bbuf, sbuf, q_ref, oacc, crun, mix_ref, *, ts):
    j = pl.program_id(1)
    nblk = ts // BLK

    @pl.when(j == 0)
    def _():
        abuf[0:A_HDR, :] = jnp.zeros((A_HDR, W_MIX), F32)
        bbuf[0:B_HDR, :] = jnp.zeros((B_HDR, W_MIX), F32)

    x = x_ref[0]
    h = _rmsnorm(x, g1_ref[...]).astype(BF16)

    pa = _dot(h, win_ref[:, 0:3 * W_MIX])
    abuf[A_HDR:A_HDR + ts, :] = pa[:, W_MIX:2 * W_MIX] * pa[:, 2 * W_MIX:3 * W_MIX]
    conv_a = jnp.zeros((ts, W_MIX), F32)
    for k in range(K_SHORT):
        off = A_HDR - (K_SHORT - 1) + k
        conv_a = conv_a + caw_ref[k:k + 1, :] * abuf[off:off + ts, :]
    mix_ref[:, 0:W_MIX] = (pa[:, 0:W_MIX] * conv_a).astype(BF16)
    abuf[0:A_HDR, :] = abuf[ts:ts + A_HDR, :]

    pb = _dot(h, win_ref[:, 3 * W_MIX:5 * W_MIX])
    bbuf[B_HDR:B_HDR + ts, :] = pb[:, 0:W_MIX] * jax.nn.sigmoid(pb[:, W_MIX:2 * W_MIX])
    n_shift_rows = ts + B_HDR - SUBLANES
    for s in range(1, SUBLANES):
        sbuf[s - 1, :, :] = bbuf[s:s + n_shift_rows, :]

    def conformer_rows(r):
        acc = jnp.zeros((CONV_ROWS, W_MIX), F32) + cbb_ref[...]
        for k in range(K_CONF):
            grp, s = divmod(B_HDR - (K_CONF - 1) + k, SUBLANES)
            off = r * CONV_ROWS + grp * SUBLANES
            src = bbuf[off:off + CONV_ROWS, :] if s == 0 else sbuf[s - 1, off:off + CONV_ROWS, :]
            acc = acc + cbw_ref[k:k + 1, :] * src
        yb = _silu(_layernorm(acc, lnbg_ref[...], lnbb_ref[...]))
        mix_ref[r * CONV_ROWS:(r + 1) * CONV_ROWS, W_MIX:2 * W_MIX] = yb.astype(BF16)

    lane_head = lax.broadcasted_iota(jnp.int32, (BLK, W_MIX), 1) // HEAD_DIM

    def head_stack(blk):
        parts = [jnp.where(lane_head == g, blk, 0.0) for g in range(N_HEADS)]
        return jnp.concatenate(parts, axis=0).astype(BF16)

    pc = _dot(h, win_ref[:, 5 * W_MIX:7 * W_MIX])
    uv = 0.5 * pc * (1.0 + lax.erf(pc * (2.0 ** -0.5)))
    u_c = uv[:, 0:W_MIX]
    v_c = _layernorm(uv[:, W_MIX:2 * W_MIX], lncg_ref[...], lncb_ref[...])
    row = lax.broadcasted_iota(jnp.int32, (BLK, BLK), 0)
    col = lax.broadcasted_iota(jnp.int32, (BLK, BLK), 1)
    ws_cat = jnp.concatenate(
        [jnp.where(col <= row, sguw_ref[g], 0.0) for g in range(N_HEADS)], axis=1).astype(BF16)
    for n in range(nblk):
        rows = slice(n * BLK, (n + 1) * BLK)
        sp = _dot(ws_cat, head_stack(v_c[rows, :])) + sgub_ref[...]
        mix_ref[rows, 2 * W_MIX:3 * W_MIX] = (u_c[rows, :] * sp).astype(BF16)

    pd = _dot(h, win_ref[:, 7 * W_MIX:10 * W_MIX])
    q_ref[...] = (pd[:, 0:W_MIX] * (HEAD_DIM ** -0.5)).astype(BF16)
    for n in range(nblk):
        rows = slice(n * BLK, (n + 1) * BLK)
        kbd_ref[j * nblk + n] = head_stack(pd[rows, W_MIX:2 * W_MIX])
        vbd_ref[j * nblk + n] = head_stack(pd[rows, 2 * W_MIX:3 * W_MIX])

    r2 = lax.broadcasted_iota(jnp.int32, (2 * BLK, 2 * BLK), 0)
    c2 = lax.broadcasted_iota(jnp.int32, (2 * BLK, 2 * BLK), 1)
    ubd = jnp.where((r2 > c2) & ((r2 < BLK) | (c2 >= BLK)), 1.0, 0.0).astype(BF16)

    t_loc = lax.broadcasted_iota(jnp.int32, (BLK, N_HEADS * BLK), 0)
    s_loc = lax.broadcasted_iota(jnp.int32, (BLK, N_HEADS * BLK), 1) % BLK
    causal = s_loc < t_loc

    def mask_own_block(v, fill):
        own = jnp.where(causal, v[0:BLK], fill)
        return own if v.shape[0] == BLK else jnp.concatenate([own, v[BLK:]], axis=0)

    def attend(q, kblk, vblk, carry, diag):
        z = _dot_nt(q, kblk)
        sp = jnp.log(1.0 + jnp.exp(-jnp.abs(z)))
        log_beta = jnp.minimum(z, 0.0) - sp
        lom = log_beta - z
        if diag:
            lom = mask_own_block(lom, 0.0)
        hi, lo = _split_bf16(lom)
        after = []
        for p in range(N_HEADS // 2):
            cols = slice(p * 2 * BLK, (p + 1) * 2 * BLK)
            after.append(_dot(hi[:, cols], ubd) + _dot(lo[:, cols], ubd))
        tot = []
        for g in range(N_HEADS):
            row_sum = jnp.sum(lom[:, g * BLK:(g + 1) * BLK], axis=-1, keepdims=True)
            tot.append(jnp.broadcast_to(row_sum, (lom.shape[0], BLK)))
        logit = log_beta + jnp.concatenate(after, axis=1) + carry
        if diag:
            logit = mask_own_block(logit, -jnp.inf)
        a = jnp.exp(logit).astype(BF16)
        return _dot(a, vblk), jnp.concatenate(tot, axis=1)

    oacc[...] = jnp.zeros((ts, W_MIX), F32)
    crun[...] = jnp.zeros((ts, N_HEADS * BLK), F32)
    conv_chunks = ts // CONV_ROWS
    for m in reversed(range(nblk)):
        rows = slice(m * BLK, ts)
        do, tot = attend(q_ref[rows, :], kbd_ref[j * nblk + m], vbd_ref[j * nblk + m],
                         crun[rows, :], True)
        oacc[rows, :] += do
        crun[rows, :] += tot
        stage = nblk - 1 - m
        for r in range(stage * conv_chunks // nblk, (stage + 1) * conv_chunks // nblk):
            conformer_rows(r)
    bbuf[0:B_HDR, :] = bbuf[ts:ts + B_HDR, :]

    def any_weight_left():
        top = jnp.max(jnp.max(crun[...], axis=0, keepdims=True), axis=1, keepdims=True)
        return (top[0, 0] > EXP_FLOOR).astype(jnp.int32)

    n_older = j * nblk

    def older_blocks(state):
        i, _ = state
        for u in range(OLDER_UNROLL):
            kb = n_older - 1 - i - u
            do, tot = attend(q_ref[...], kbd_ref[kb], vbd_ref[kb], crun[...], False)
            oacc[...] += do
            crun[...] += tot
        return i + OLDER_UNROLL, any_weight_left()

    lax.while_loop(lambda s: (s[0] < n_older) & (s[1] > 0), older_blocks,
                   (jnp.int32(0), any_weight_left()))
    mix_ref[:, 3 * W_MIX:4 * W_MIX] = oacc[...].astype(BF16)

    o_ref[0] = x + _dot(mix_ref[...], wout_ref[...])


def _ffn_kernel(x_ref, g2_ref, wup_ref, cfw_ref, wdn_ref, gf_ref, o_ref, gbuf, vbuf,
                *, tm, d_ff, fc, final_norm):
    j = pl.program_id(1)
    nc = d_ff // fc
    hdr = SUBLANES

    @pl.when(j == 0)
    def _():
        gbuf[:, 0:hdr, :] = jnp.zeros((nc, hdr, fc), F32)
        vbuf[:, 0:hdr, :] = jnp.zeros((nc, hdr, fc), F32)

    x = x_ref[0]
    h2 = _rmsnorm(x, g2_ref[...]).astype(BF16)
    acc = x

    def up_proj(c):
        for buf, base in ((gbuf, 0), (vbuf, d_ff)):
            cols = slice(base + c * fc, base + (c + 1) * fc)
            buf[c, hdr:hdr + tm, :] = _dot(h2, wup_ref[:, cols])

    def conv_gate(c):
        halves = []
        for buf, base in ((gbuf, 0), (vbuf, d_ff)):
            cols = slice(base + c * fc, base + (c + 1) * fc)
            conv = jnp.zeros((tm, fc), F32)
            for k in range(K_FFN):
                off = hdr - (K_FFN - 1) + k
                conv = conv + cfw_ref[k:k + 1, cols] * buf[c, off:off + tm, :]
            buf[c, 0:hdr, :] = buf[c, tm:tm + hdr, :]
            halves.append(conv)
        return (_silu(halves[0]) * halves[1]).astype(BF16)

    acts = {}
    for i in range(nc + 2):
        if i < nc:
            up_proj(i)
        if 0 <= i - 1 < nc:
            acts[i - 1] = conv_gate(i - 1)
        if 0 <= i - 2 < nc:
            acc = acc + _dot(acts.pop(i - 2), wdn_ref[(i - 2) * fc:(i - 1) * fc, :])
    if final_norm:
        acc = _rmsnorm(acc, gf_ref[...])
    o_ref[0] = acc


def _full(shape):
    return pl.BlockSpec(shape, lambda b, j: (0,) * len(shape), pipeline_mode=pl.Buffered(1))


def _mixer_call(x, g1, w_in, caw, cbw, cbb, lnbg, lnbb, lncg, lncb, sguw, sgub, w_out, *, ts):
    bsz, s_len, d = x.shape
    tile = pl.BlockSpec((1, ts, d), lambda b, j: (b, j, 0))
    args = (x, g1, w_in, caw, cbw, cbb, lnbg, lnbb, lncg, lncb, sguw, sgub, w_out)
    in_specs = [tile] + [_full(a.shape) for a in args[1:]]
    return pl.pallas_call(
        functools.partial(_mixer_kernel, ts=ts),
        grid=(bsz, s_len // ts),
        in_specs=in_specs,
        out_specs=tile,
        out_shape=jax.ShapeDtypeStruct(x.shape, x.dtype),
        scratch_shapes=[
            pltpu.VMEM((s_len // BLK, N_HEADS * BLK, W_MIX), BF16),
            pltpu.VMEM((s_len // BLK, N_HEADS * BLK, W_MIX), BF16),
            pltpu.VMEM((A_HDR + ts, W_MIX), F32),
            pltpu.VMEM((B_HDR + ts, W_MIX), F32),
            pltpu.VMEM((SUBLANES - 1, ts + B_HDR - SUBLANES, W_MIX), F32),
            pltpu.VMEM((ts, W_MIX), BF16),
            pltpu.VMEM((ts, W_MIX), F32),
            pltpu.VMEM((ts, N_HEADS * BLK), F32),
            pltpu.VMEM((ts, d), BF16),
        ],
        compiler_params=pltpu.CompilerParams(
            dimension_semantics=("arbitrary", "arbitrary"),
            vmem_limit_bytes=VMEM_LIMIT_BYTES),
        name="mixer",
    )(*args)


def _ffn_call(x, g2, w_up, cfw, w_dn, gf, *, tm, fc, final_norm):
    bsz, s_len, d = x.shape
    d_ff = w_dn.shape[0]
    tile = pl.BlockSpec((1, tm, d), lambda b, j: (b, j, 0))
    args = (x, g2, w_up, cfw, w_dn, gf)
    in_specs = [tile] + [_full(a.shape) for a in args[1:]]
    return pl.pallas_call(
        functools.partial(_ffn_kernel, tm=tm, d_ff=d_ff, fc=fc, final_norm=final_norm),
        grid=(bsz, s_len // tm),
        in_specs=in_specs,
        out_specs=tile,
        out_shape=jax.ShapeDtypeStruct(x.shape, x.dtype),
        scratch_shapes=[
            pltpu.VMEM((d_ff // fc, SUBLANES + tm, fc), F32),
            pltpu.VMEM((d_ff // fc, SUBLANES + tm, fc), F32),
        ],
        compiler_params=pltpu.CompilerParams(
            dimension_semantics=("arbitrary", "arbitrary"),
            vmem_limit_bytes=VMEM_LIMIT_BYTES),
        name="ffn",
    )(*args)


def kernel(x, norm1_g, w_in, conv_a_w, conv_b_w, conv_b_b, ln_b_g, ln_b_b, ln_c_g, ln_c_b,
           sgu_w, sgu_b, w_out, norm2_g, w_up, conv_f_w, w_down, final_g):
    depth = w_in.shape[0]
    row = lambda v: v[None, :]
    for l in range(depth):
        sgub = jnp.repeat(sgu_b[l].T, HEAD_DIM, axis=1)
        x = _mixer_call(
            x, row(norm1_g[l]), w_in[l].astype(BF16), conv_a_w[l], conv_b_w[l], row(conv_b_b[l]),
            row(ln_b_g[l]), row(ln_b_b[l]), row(ln_c_g[l]), row(ln_c_b[l]), sgu_w[l], sgub,
            w_out[l].astype(BF16), ts=512)
        x = _ffn_call(
            x, row(norm2_g[l]), w_up[l].astype(BF16), conv_f_w[l], w_down[l].astype(BF16),
            row(final_g), tm=512, fc=256, final_norm=(l == depth - 1))
    return x
```

```python
import functools

import jax
import jax.numpy as jnp
from jax import lax
from jax.experimental import pallas as pl
from jax.experimental.pallas import tpu as pltpu

F32 = jnp.float32
BF16 = jnp.bfloat16

HEAD_DIM = 64
N_HEADS = 4
W_MIX = N_HEADS * HEAD_DIM
BLK = 128
K_SHORT = 3
K_CONF = 31
K_FFN = 3
RMS_EPS = 1e-6
LN_EPS = 1e-5
SUBLANES = 8
LOG2_E = 1.4426950408889634
A_HDR = SUBLANES
B_HDR = 4 * SUBLANES
CONV_ROWS = 64
VMEM_LIMIT_BYTES = 56 * 1024 * 1024
EXP_FLOOR = -110.0
OLDER_UNROLL = 2


def _dot(a, b):
    return jnp.dot(a, b, preferred_element_type=F32)


def _dot_nt(a, b):
    return lax.dot_general(a, b, (((1,), (1,)), ((), ())), preferred_element_type=F32)


def _rmsnorm(x, g):
    return x * lax.rsqrt(jnp.mean(x * x, axis=-1, keepdims=True) + RMS_EPS) * g


def _layernorm(x, g, b):
    mu = jnp.mean(x, axis=-1, keepdims=True)
    xc = x - mu
    var = jnp.mean(xc * xc, axis=-1, keepdims=True)
    return xc * lax.rsqrt(var + LN_EPS) * g + b


def _silu(x):
    return x * jax.nn.sigmoid(x)


def _split_bf16(x):
    hi = x.astype(BF16)
    lo = (x - hi.astype(F32)).astype(BF16)
    return hi, lo


def _mixer_kernel(x_ref, g1_ref, win_ref, caw_ref, cbw_ref, cbb_ref, lnbg_ref, lnbb_ref,
                  lncg_ref, lncb_ref, sguw_ref, sgub_ref, wout_ref, o_ref,
                  kbd_ref, vbd_ref, abuf, bbuf, sbuf, q_ref, oacc, crun, mix_ref, *, ts):
    j = pl.program_id(1)
    nblk = ts // BLK

    @pl.when(j == 0)
    def _():
        abuf[0:A_HDR, :] = jnp.zeros((A_HDR, W_MIX), F32)
        bbuf[0:B_HDR, :] = jnp.zeros((B_HDR, W_MIX), F32)

    x = x_ref[0]
    h = _rmsnorm(x, g1_ref[...]).astype(BF16)

    lane_head = lax.broadcasted_iota(jnp.int32, (BLK, W_MIX), 1) // HEAD_DIM

    def head_stack(blk):
        parts = [jnp.where(lane_head == g, blk, 0.0) for g in range(N_HEADS)]
        return jnp.concatenate(parts, axis=0).astype(BF16)

    pd = _dot(h, win_ref[:, 7 * W_MIX:10 * W_MIX])
    q_ref[...] = (pd[:, 0:W_MIX] * (HEAD_DIM ** -0.5)).astype(BF16)
    for n in range(nblk):
        rows = slice(n * BLK, (n + 1) * BLK)
        kbd_ref[j * nblk + n] = head_stack(pd[rows, W_MIX:2 * W_MIX])
        vbd_ref[j * nblk + n] = head_stack(pd[rows, 2 * W_MIX:3 * W_MIX])

    pb = _dot(h, win_ref[:, 3 * W_MIX:5 * W_MIX])
    bbuf[B_HDR:B_HDR + ts, :] = pb[:, 0:W_MIX] * jax.nn.sigmoid(pb[:, W_MIX:2 * W_MIX])
    n_shift_rows = ts + B_HDR - SUBLANES
    for s in range(1, SUBLANES):
        sbuf[s - 1, :, :] = bbuf[s:s + n_shift_rows, :]

    def conformer_rows(r):
        acc = jnp.zeros((CONV_ROWS, W_MIX), F32) + cbb_ref[...]
        for k in range(K_CONF):
            grp, s = divmod(B_HDR - (K_CONF - 1) + k, SUBLANES)
            off = r * CONV_ROWS + grp * SUBLANES
            src = bbuf[off:off + CONV_ROWS, :] if s == 0 else sbuf[s - 1, off:off + CONV_ROWS, :]
            acc = acc + cbw_ref[k:k + 1, :] * src
        yb = _silu(_layernorm(acc, lnbg_ref[...], lnbb_ref[...]))
        mix_ref[r * CONV_ROWS:(r + 1) * CONV_ROWS, W_MIX:2 * W_MIX] = yb.astype(BF16)

    pa = _dot(h, win_ref[:, 0:3 * W_MIX])
    abuf[A_HDR:A_HDR + ts, :] = pa[:, W_MIX:2 * W_MIX] * pa[:, 2 * W_MIX:3 * W_MIX]
    conv_a = jnp.zeros((ts, W_MIX), F32)
    for k in range(K_SHORT):
        off = A_HDR - (K_SHORT - 1) + k
        conv_a = conv_a + caw_ref[k:k + 1, :] * abuf[off:off + ts, :]
    mix_ref[:, 0:W_MIX] = (pa[:, 0:W_MIX] * conv_a).astype(BF16)
    abuf[0:A_HDR, :] = abuf[ts:ts + A_HDR, :]
    out = x + _dot(mix_ref[:, 0:W_MIX], wout_ref[0:W_MIX, :])

    pc = _dot(h, win_ref[:, 5 * W_MIX:7 * W_MIX])
    uv = 0.5 * pc * (1.0 + lax.erf(pc * (2.0 ** -0.5)))
    u_c = uv[:, 0:W_MIX]
    v_c = _layernorm(uv[:, W_MIX:2 * W_MIX], lncg_ref[...], lncb_ref[...])
    row = lax.broadcasted_iota(jnp.int32, (BLK, BLK), 0)
    col = lax.broadcasted_iota(jnp.int32, (BLK, BLK), 1)
    ws_cat = jnp.concatenate(
        [jnp.where(col <= row, sguw_ref[g], 0.0) for g in range(N_HEADS)], axis=1).astype(BF16)
    for n in range(nblk):
        rows = slice(n * BLK, (n + 1) * BLK)
        sp = _dot(ws_cat, head_stack(v_c[rows, :])) + sgub_ref[...]
        mix_ref[rows, 2 * W_MIX:3 * W_MIX] = (u_c[rows, :] * sp).astype(BF16)
    out = out + _dot(mix_ref[:, 2 * W_MIX:3 * W_MIX], wout_ref[2 * W_MIX:3 * W_MIX, :])

    r2 = lax.broadcasted_iota(jnp.int32, (2 * BLK, 2 * BLK), 0)
    c2 = lax.broadcasted_iota(jnp.int32, (2 * BLK, 2 * BLK), 1)
    ubd = jnp.where((r2 > c2) & ((r2 < BLK) | (c2 >= BLK)), 1.0, 0.0).astype(BF16)

    t_loc = lax.broadcasted_iota(jnp.int32, (BLK, N_HEADS * BLK), 0)
    s_loc = lax.broadcasted_iota(jnp.int32, (BLK, N_HEADS * BLK), 1) % BLK
    causal = s_loc < t_loc

    def mask_own_block(v, fill):
        own = jnp.where(causal, v[0:BLK], fill)
        return own if v.shape[0] == BLK else jnp.concatenate([own, v[BLK:]], axis=0)

    def attend(q, kblk, vblk, carry, diag):
        z = _dot_nt(q, kblk)
        sp = jnp.log(1.0 + jnp.exp2(jnp.abs(z) * -LOG2_E))
        log_beta = jnp.minimum(z, 0.0) - sp
        lom = log_beta - z
        if diag:
            lom = mask_own_block(lom, 0.0)
        hi, lo = _split_bf16(lom)
        after = []
        for p in range(N_HEADS // 2):
            cols = slice(p * 2 * BLK, (p + 1) * 2 * BLK)
            after.append(_dot(hi[:, cols], ubd) + _dot(lo[:, cols], ubd))
        tot = []
        for g in range(N_HEADS):
            row_sum = jnp.sum(lom[:, g * BLK:(g + 1) * BLK], axis=-1, keepdims=True)
            tot.append(jnp.broadcast_to(row_sum, (lom.shape[0], BLK)))
        logit = log_beta + jnp.concatenate(after, axis=1) + carry
        if diag:
            logit = mask_own_block(logit, -jnp.inf)
        a = jnp.exp(logit).astype(BF16)
        return _dot(a, vblk), jnp.concatenate(tot, axis=1)

    oacc[...] = jnp.zeros((ts, W_MIX), F32)
    crun[...] = jnp.zeros((ts, N_HEADS * BLK), F32)
    conv_chunks = ts // CONV_ROWS
    for m in reversed(range(nblk)):
        rows = slice(m * BLK, ts)
        do, tot = attend(q_ref[rows, :], kbd_ref[j * nblk + m], vbd_ref[j * nblk + m],
                         crun[rows, :], True)
        oacc[rows, :] += do
        crun[rows, :] += tot
        stage = nblk - 1 - m
        for r in range(stage * conv_chunks // nblk, (stage + 1) * conv_chunks // nblk):
            conformer_rows(r)
    bbuf[0:B_HDR, :] = bbuf[ts:ts + B_HDR, :]
    o_ref[0] = out + _dot(mix_ref[:, W_MIX:2 * W_MIX], wout_ref[W_MIX:2 * W_MIX, :])

    def any_weight_left():
        top = jnp.max(jnp.max(crun[...], axis=0, keepdims=True), axis=1, keepdims=True)
        return (top[0, 0] > EXP_FLOOR).astype(jnp.int32)

    n_older = j * nblk

    def older_blocks(state):
        i, _ = state
        for u in range(OLDER_UNROLL):
            kb = n_older - 1 - i - u
            do, tot = attend(q_ref[...], kbd_ref[kb], vbd_ref[kb], crun[...], False)
            oacc[...] += do
            crun[...] += tot
        return i + OLDER_UNROLL, any_weight_left()

    lax.while_loop(lambda s: (s[0] < n_older) & (s[1] > 0), older_blocks,
                   (jnp.int32(0), any_weight_left()))
    o_ref[0] += _dot(oacc[...].astype(BF16), wout_ref[3 * W_MIX:4 * W_MIX, :])


def _ffn_kernel(x_ref, g2_ref, wup_ref, cfw_ref, wdn_ref, gf_ref, o_ref, gbuf, vbuf,
                *, tm, d_ff, fc, final_norm):
    j = pl.program_id(1)
    nc = d_ff // fc
    hdr = SUBLANES

    @pl.when(j == 0)
    def _():
        gbuf[:, 0:hdr, :] = jnp.zeros((nc, hdr, fc), F32)
        vbuf[:, 0:hdr, :] = jnp.zeros((nc, hdr, fc), F32)

    row_halves = (slice(0, tm // 2), slice(tm // 2, tm))
    whole_tile = (slice(0, tm),)
    x = x_ref[0]
    h2_halves = [_rmsnorm(x[rows], g2_ref[...]).astype(BF16) for rows in row_halves]
    h2 = jnp.concatenate(h2_halves, axis=0)

    def up_proj(c, rows):
        lhs = h2 if rows.stop - rows.start == tm else h2_halves[row_halves.index(rows)]
        for buf, base in ((gbuf, 0), (vbuf, d_ff)):
            cols = slice(base + c * fc, base + (c + 1) * fc)
            buf[c, hdr + rows.start:hdr + rows.stop, :] = _dot(lhs, wup_ref[:, cols])

    sub_idx = lax.broadcasted_iota(jnp.int32, (hdr, fc), 0)

    def conv_gate(c, rows):
        halves = []
        for buf, base in ((gbuf, 0), (vbuf, d_ff)):
            cols = slice(base + c * fc, base + (c + 1) * fc)
            u = buf[c, hdr + rows.start:hdr + rows.stop, :]
            prev = buf[c, rows.start:rows.start + hdr, :]
            conv = cfw_ref[K_FFN - 1:K_FFN, cols] * u
            for delay in range(1, K_FFN):
                sh = pltpu.roll(u, delay, axis=0)
                first = jnp.where(sub_idx < delay, pltpu.roll(prev, delay, axis=0), sh[0:hdr])
                sh = jnp.concatenate([first, sh[hdr:]], axis=0)
                conv = conv + cfw_ref[K_FFN - 1 - delay:K_FFN - delay, cols] * sh
            if rows.stop == tm:
                buf[c, 0:hdr, :] = u[rows.stop - rows.start - hdr:]
            halves.append(conv)
        return (_silu(halves[0]) * halves[1]).astype(BF16)

    def split(c):
        return row_halves if c in (0, nc - 1) else whole_tile

    acc = x
    acts = {}
    for i in range(nc + 2):
        if i < nc:
            for rows in (row_halves if i == 0 else whole_tile):
                up_proj(i, rows)
        if 0 <= i - 1 < nc:
            acts[i - 1] = [conv_gate(i - 1, rows) for rows in split(i - 1)]
        if 0 <= i - 2 < nc:
            c = i - 2
            parts = acts.pop(c)
            if c == nc - 1:
                for rows, act in zip(row_halves, parts):
                    out = acc[rows] + _dot(act, wdn_ref[c * fc:(c + 1) * fc, :])
                    if final_norm:
                        out = _rmsnorm(out, gf_ref[...])
                    o_ref[0, rows, :] = out
            else:
                act = parts[0] if len(parts) == 1 else jnp.concatenate(parts, axis=0)
                acc = acc + _dot(act, wdn_ref[c * fc:(c + 1) * fc, :])


def _full(shape):
    return pl.BlockSpec(shape, lambda b, j: (0,) * len(shape), pipeline_mode=pl.Buffered(1))


def _mixer_call(x, g1, w_in, caw, cbw, cbb, lnbg, lnbb, lncg, lncb, sguw, sgub, w_out, *, ts):
    bsz, s_len, d = x.shape
    tile = pl.BlockSpec((1, ts, d), lambda b, j: (b, j, 0))
    args = (x, g1, w_in, caw, cbw, cbb, lnbg, lnbb, lncg, lncb, sguw, sgub, w_out)
    in_specs = [tile] + [_full(a.shape) for a in args[1:]]
    return pl.pallas_call(
        functools.partial(_mixer_kernel, ts=ts),
        grid=(bsz, s_len // ts),
        in_specs=in_specs,
        out_specs=tile,
        out_shape=jax.ShapeDtypeStruct(x.shape, x.dtype),
        scratch_shapes=[
            pltpu.VMEM((s_len // BLK, N_HEADS * BLK, W_MIX), BF16),
            pltpu.VMEM((s_len // BLK, N_HEADS * BLK, W_MIX), BF16),
            pltpu.VMEM((A_HDR + ts, W_MIX), F32),
            pltpu.VMEM((B_HDR + ts, W_MIX), F32),
            pltpu.VMEM((SUBLANES - 1, ts + B_HDR - SUBLANES, W_MIX), F32),
            pltpu.VMEM((ts, W_MIX), BF16),
            pltpu.VMEM((ts, W_MIX), F32),
            pltpu.VMEM((ts, N_HEADS * BLK), F32),
            pltpu.VMEM((ts, 3 * W_MIX), BF16),
        ],
        compiler_params=pltpu.CompilerParams(
            dimension_semantics=("arbitrary", "arbitrary"),
            vmem_limit_bytes=VMEM_LIMIT_BYTES),
        name="mixer",
    )(*args)


def _ffn_call(x, g2, w_up, cfw, w_dn, gf, *, tm, fc, final_norm):
    bsz, s_len, d = x.shape
    d_ff = w_dn.shape[0]
    tile = pl.BlockSpec((1, tm, d), lambda b, j: (b, j, 0))
    args = (x, g2, w_up, cfw, w_dn, gf)
    in_specs = [tile] + [_full(a.shape) for a in args[1:]]
    return pl.pallas_call(
        functools.partial(_ffn_kernel, tm=tm, d_ff=d_ff, fc=fc, final_norm=final_norm),
        grid=(bsz, s_len // tm),
        in_specs=in_specs,
        out_specs=tile,
        out_shape=jax.ShapeDtypeStruct(x.shape, x.dtype),
        scratch_shapes=[
            pltpu.VMEM((d_ff // fc, SUBLANES + tm, fc), F32),
            pltpu.VMEM((d_ff // fc, SUBLANES + tm, fc), F32),
        ],
        compiler_params=pltpu.CompilerParams(
            dimension_semantics=("arbitrary", "arbitrary"),
            vmem_limit_bytes=VMEM_LIMIT_BYTES),
        name="ffn",
    )(*args)


def kernel(x, norm1_g, w_in, conv_a_w, conv_b_w, conv_b_b, ln_b_g, ln_b_b, ln_c_g, ln_c_b,
           sgu_w, sgu_b, w_out, norm2_g, w_up, conv_f_w, w_down, final_g):
    depth = w_in.shape[0]
    row = lambda v: v[None, :]
    for l in range(depth):
        sgub = jnp.repeat(sgu_b[l].T, HEAD_DIM, axis=1)
        x = _mixer_call(
            x, row(norm1_g[l]), w_in[l].astype(BF16), conv_a_w[l], conv_b_w[l], row(conv_b_b[l]),
            row(ln_b_g[l]), row(ln_b_b[l]), row(ln_c_g[l]), row(ln_c_b[l]), sgu_w[l], sgub,
            w_out[l].astype(BF16), ts=512)
        x = _ffn_call(
            x, row(norm2_g[l]), w_up[l].astype(BF16), conv_f_w[l], w_down[l].astype(BF16),
            row(final_g), tm=512, fc=256, final_norm=(l == depth - 1))
    return x
```

```python
import functools

import jax
import jax.numpy as jnp
from jax import lax
from jax.experimental import pallas as pl
from jax.experimental.pallas import tpu as pltpu

F32 = jnp.float32
BF16 = jnp.bfloat16

HEAD_DIM = 64
N_HEADS = 4
W_MIX = N_HEADS * HEAD_DIM
BLK = 128
K_SHORT = 3
K_CONF = 31
K_FFN = 3
RMS_EPS = 1e-6
LN_EPS = 1e-5
SUBLANES = 8
LOG2_E = 1.4426950408889634
A_HDR = SUBLANES
B_HDR = 4 * SUBLANES
CONV_ROWS = 64
VMEM_LIMIT_BYTES = 56 * 1024 * 1024
EXP_FLOOR = -110.0
OLDER_UNROLL = 2


def _dot(a, b):
    return jnp.dot(a, b, preferred_element_type=F32)


def _dot_nt(a, b):
    return lax.dot_general(a, b, (((1,), (1,)), ((), ())), preferred_element_type=F32)


def _rmsnorm(x, g):
    return x * lax.rsqrt(jnp.mean(x * x, axis=-1, keepdims=True) + RMS_EPS) * g


def _layernorm(x, g, b):
    mu = jnp.mean(x, axis=-1, keepdims=True)
    xc = x - mu
    var = jnp.mean(xc * xc, axis=-1, keepdims=True)
    return xc * lax.rsqrt(var + LN_EPS) * g + b


def _silu(x):
    return x * jax.nn.sigmoid(x)


def _split_bf16(x):
    hi = x.astype(BF16)
    lo = (x - hi.astype(F32)).astype(BF16)
    return hi, lo


def _mixer_kernel(x_ref, g1_ref, win_ref, caw_ref, cbw_ref, cbb_ref, lnbg_ref, lnbb_ref,
                  lncg_ref, lncb_ref, sguw_ref, sgub_ref, wout_ref, o_ref,
                  kbd_ref, vbd_ref, abuf, bbuf, sbuf, q_ref, oacc, crun, mix_ref, *, ts):
    j = pl.program_id(1)
    nblk = ts // BLK

    @pl.when(j == 0)
    def _():
        abuf[0:A_HDR, :] = jnp.zeros((A_HDR, W_MIX), F32)
        bbuf[0:B_HDR, :] = jnp.zeros((B_HDR, W_MIX), F32)

    x = x_ref[0]
    h = _rmsnorm(x, g1_ref[...]).astype(BF16)

    lane_head = lax.broadcasted_iota(jnp.int32, (BLK, W_MIX), 1) // HEAD_DIM

    def head_stack(blk):
        parts = [jnp.where(lane_head == g, blk, 0.0) for g in range(N_HEADS)]
        return jnp.concatenate(parts, axis=0).astype(BF16)

    pb = _dot(h, win_ref[:, 3 * W_MIX:5 * W_MIX])
    bbuf[B_HDR:B_HDR + ts, :] = pb[:, 0:W_MIX] * jax.nn.sigmoid(pb[:, W_MIX:2 * W_MIX])
    n_shift_rows = ts + B_HDR - SUBLANES
    for s in range(1, SUBLANES):
        sbuf[s - 1, :, :] = bbuf[s:s + n_shift_rows, :]

    def conformer_rows(r):
        acc = jnp.zeros((CONV_ROWS, W_MIX), F32) + cbb_ref[...]
        for k in range(K_CONF):
            grp, s = divmod(B_HDR - (K_CONF - 1) + k, SUBLANES)
            off = r * CONV_ROWS + grp * SUBLANES
            src = bbuf[off:off + CONV_ROWS, :] if s == 0 else sbuf[s - 1, off:off + CONV_ROWS, :]
            acc = acc + cbw_ref[k:k + 1, :] * src
        yb = _silu(_layernorm(acc, lnbg_ref[...], lnbb_ref[...]))
        mix_ref[r * CONV_ROWS:(r + 1) * CONV_ROWS, W_MIX:2 * W_MIX] = yb.astype(BF16)

    pd = _dot(h, win_ref[:, 7 * W_MIX:10 * W_MIX])
    q_ref[...] = (pd[:, 0:W_MIX] * (HEAD_DIM ** -0.5)).astype(BF16)
    for n in range(nblk):
        rows = slice(n * BLK, (n + 1) * BLK)
        kbd_ref[j * nblk + n] = head_stack(pd[rows, W_MIX:2 * W_MIX])
        vbd_ref[j * nblk + n] = head_stack(pd[rows, 2 * W_MIX:3 * W_MIX])

    r2 = lax.broadcasted_iota(jnp.int32, (2 * BLK, 2 * BLK), 0)
    c2 = lax.broadcasted_iota(jnp.int32, (2 * BLK, 2 * BLK), 1)
    ubd = jnp.where((r2 > c2) & ((r2 < BLK) | (c2 >= BLK)), 1.0, 0.0).astype(BF16)

    t_loc = lax.broadcasted_iota(jnp.int32, (BLK, N_HEADS * BLK), 0)
    s_loc = lax.broadcasted_iota(jnp.int32, (BLK, N_HEADS * BLK), 1) % BLK
    causal = s_loc < t_loc

    def mask_own_block(v, fill):
        own = jnp.where(causal, v[0:BLK], fill)
        return own if v.shape[0] == BLK else jnp.concatenate([own, v[BLK:]], axis=0)

    def attend(q, kblk, vblk, carry, diag):
        z = _dot_nt(q, kblk)
        sp = jnp.log(1.0 + jnp.exp2(jnp.abs(z) * -LOG2_E))
        log_beta = jnp.minimum(z, 0.0) - sp
        lom = log_beta - z
        if diag:
            lom = mask_own_block(lom, 0.0)
        hi, lo = _split_bf16(lom)
        after = []
        for p in range(N_HEADS // 2):
            cols = slice(p * 2 * BLK, (p + 1) * 2 * BLK)
            after.append(_dot(hi[:, cols], ubd) + _dot(lo[:, cols], ubd))
        tot = []
        for g in range(N_HEADS):
            row_sum = jnp.sum(lom[:, g * BLK:(g + 1) * BLK], axis=-1, keepdims=True)
            tot.append(jnp.broadcast_to(row_sum, (lom.shape[0], BLK)))
        logit = log_beta + jnp.concatenate(after, axis=1) + carry
        if diag:
            logit = mask_own_block(logit, -jnp.inf)
        a = jnp.exp(logit).astype(BF16)
        return _dot(a, vblk), jnp.concatenate(tot, axis=1)

    oacc[...] = jnp.zeros((ts, W_MIX), F32)
    crun[...] = jnp.zeros((ts, N_HEADS * BLK), F32)

    def own_tile_stage(m):
        rows = slice(m * BLK, ts)
        do, tot = attend(q_ref[rows, :], kbd_ref[j * nblk + m], vbd_ref[j * nblk + m],
                         crun[rows, :], True)
        oacc[rows, :] += do
        crun[rows, :] += tot

    conv_chunks = ts // CONV_ROWS
    for r in range(0, conv_chunks // 4):
        conformer_rows(r)

    pa = _dot(h, win_ref[:, 0:3 * W_MIX])
    abuf[A_HDR:A_HDR + ts, :] = pa[:, W_MIX:2 * W_MIX] * pa[:, 2 * W_MIX:3 * W_MIX]
    conv_a = jnp.zeros((ts, W_MIX), F32)
    for k in range(K_SHORT):
        off = A_HDR - (K_SHORT - 1) + k
        conv_a = conv_a + caw_ref[k:k + 1, :] * abuf[off:off + ts, :]
    mix_ref[:, 0:W_MIX] = (pa[:, 0:W_MIX] * conv_a).astype(BF16)
    abuf[0:A_HDR, :] = abuf[ts:ts + A_HDR, :]
    out = x + _dot(mix_ref[:, 0:W_MIX], wout_ref[0:W_MIX, :])
    for r in range(conv_chunks // 4, 5 * conv_chunks // 8):
        conformer_rows(r)

    pc = _dot(h, win_ref[:, 5 * W_MIX:7 * W_MIX])
    uv = 0.5 * pc * (1.0 + lax.erf(pc * (2.0 ** -0.5)))
    u_c = uv[:, 0:W_MIX]
    v_c = _layernorm(uv[:, W_MIX:2 * W_MIX], lncg_ref[...], lncb_ref[...])
    row = lax.broadcasted_iota(jnp.int32, (BLK, BLK), 0)
    col = lax.broadcasted_iota(jnp.int32, (BLK, BLK), 1)
    ws_cat = jnp.concatenate(
        [jnp.where(col <= row, sguw_ref[g], 0.0) for g in range(N_HEADS)], axis=1).astype(BF16)
    for n in range(nblk):
        rows = slice(n * BLK, (n + 1) * BLK)
        sp = _dot(ws_cat, head_stack(v_c[rows, :])) + sgub_ref[...]
        mix_ref[rows, 2 * W_MIX:3 * W_MIX] = (u_c[rows, :] * sp).astype(BF16)
    out = out + _dot(mix_ref[:, 2 * W_MIX:3 * W_MIX], wout_ref[2 * W_MIX:3 * W_MIX, :])
    for r in range(5 * conv_chunks // 8, conv_chunks):
        conformer_rows(r)
    bbuf[0:B_HDR, :] = bbuf[ts:ts + B_HDR, :]
    out = out + _dot(mix_ref[:, W_MIX:2 * W_MIX], wout_ref[W_MIX:2 * W_MIX, :])

    for m in reversed(range(nblk)):
        own_tile_stage(m)
    o_ref[0] = out

    def any_weight_left():
        top = jnp.max(jnp.max(crun[...], axis=0, keepdims=True), axis=1, keepdims=True)
        return (top[0, 0] > EXP_FLOOR).astype(jnp.int32)

    n_older = j * nblk

    def older_blocks(state):
        i, _ = state
        for u in range(OLDER_UNROLL):
            kb = n_older - 1 - i - u
            do, tot = attend(q_ref[...], kbd_ref[kb], vbd_ref[kb], crun[...], False)
            oacc[...] += do
            crun[...] += tot
        return i + OLDER_UNROLL, any_weight_left()

    lax.while_loop(lambda s: (s[0] < n_older) & (s[1] > 0), older_blocks,
                   (jnp.int32(0), any_weight_left()))
    o_ref[0] += _dot(oacc[...].astype(BF16), wout_ref[3 * W_MIX:4 * W_MIX, :])


def _ffn_kernel(x_ref, g2_ref, wup_ref, cfw_ref, wdn_ref, gf_ref, o_ref, gbuf, vbuf,
                *, tm, d_ff, fc, final_norm):
    j = pl.program_id(1)
    nc = d_ff // fc
    hdr = SUBLANES

    @pl.when(j == 0)
    def _():
        gbuf[:, 0:hdr, :] = jnp.zeros((nc, hdr, fc), F32)
        vbuf[:, 0:hdr, :] = jnp.zeros((nc, hdr, fc), F32)

    row_halves = (slice(0, tm // 2), slice(tm // 2, tm))
    whole_tile = (slice(0, tm),)
    x = x_ref[0]
    h2_halves = [_rmsnorm(x[rows], g2_ref[...]).astype(BF16) for rows in row_halves]
    h2 = jnp.concatenate(h2_halves, axis=0)

    def up_proj(c, rows):
        lhs = h2 if rows.stop - rows.start == tm else h2_halves[row_halves.index(rows)]
        for buf, base in ((gbuf, 0), (vbuf, d_ff)):
            cols = slice(base + c * fc, base + (c + 1) * fc)
            buf[c, hdr + rows.start:hdr + rows.stop, :] = _dot(lhs, wup_ref[:, cols])

    sub_idx = lax.broadcasted_iota(jnp.int32, (hdr, fc), 0)

    def conv_gate(c, rows):
        halves = []
        for buf, base in ((gbuf, 0), (vbuf, d_ff)):
            cols = slice(base + c * fc, base + (c + 1) * fc)
            u = buf[c, hdr + rows.start:hdr + rows.stop, :]
            prev = buf[c, rows.start:rows.start + hdr, :]
            conv = cfw_ref[K_FFN - 1:K_FFN, cols] * u
            for delay in range(1, K_FFN):
                sh = pltpu.roll(u, delay, axis=0)
                first = jnp.where(sub_idx < delay, pltpu.roll(prev, delay, axis=0), sh[0:hdr])
                sh = jnp.concatenate([first, sh[hdr:]], axis=0)
                conv = conv + cfw_ref[K_FFN - 1 - delay:K_FFN - delay, cols] * sh
            if rows.stop == tm:
                buf[c, 0:hdr, :] = u[rows.stop - rows.start - hdr:]
            halves.append(conv)
        return (_silu(halves[0]) * halves[1]).astype(BF16)

    def split(c):
        return row_halves if c in (0, nc - 1) else whole_tile

    acc = x
    acts = {}
    for i in range(nc + 2):
        if i < nc:
            for rows in (row_halves if i == 0 else whole_tile):
                up_proj(i, rows)
        if 0 <= i - 1 < nc:
            acts[i - 1] = [conv_gate(i - 1, rows) for rows in split(i - 1)]
        if 0 <= i - 2 < nc:
            c = i - 2
            parts = acts.pop(c)
            if c == nc - 1:
                for rows, act in zip(row_halves, parts):
                    out = acc[rows] + _dot(act, wdn_ref[c * fc:(c + 1) * fc, :])
                    if final_norm:
                        out = _rmsnorm(out, gf_ref[...])
                    o_ref[0, rows, :] = out
            else:
                act = parts[0] if len(parts) == 1 else jnp.concatenate(parts, axis=0)
                acc = acc + _dot(act, wdn_ref[c * fc:(c + 1) * fc, :])


def _full(shape):
    return pl.BlockSpec(shape, lambda b, j: (0,) * len(shape), pipeline_mode=pl.Buffered(1))


def _mixer_call(x, g1, w_in, caw, cbw, cbb, lnbg, lnbb, lncg, lncb, sguw, sgub, w_out, *, ts):
    bsz, s_len, d = x.shape
    tile = pl.BlockSpec((1, ts, d), lambda b, j: (b, j, 0))
    args = (x, g1, w_in, caw, cbw, cbb, lnbg, lnbb, lncg, lncb, sguw, sgub, w_out)
    in_specs = [tile] + [_full(a.shape) for a in args[1:]]
    return pl.pallas_call(
        functools.partial(_mixer_kernel, ts=ts),
        grid=(bsz, s_len // ts),
        in_specs=in_specs,
        out_specs=tile,
        out_shape=jax.ShapeDtypeStruct(x.shape, x.dtype),
        scratch_shapes=[
            pltpu.VMEM((s_len // BLK, N_HEADS * BLK, W_MIX), BF16),
            pltpu.VMEM((s_len // BLK, N_HEADS * BLK, W_MIX), BF16),
            pltpu.VMEM((A_HDR + ts, W_MIX), F32),
            pltpu.VMEM((B_HDR + ts, W_MIX), F32),
            pltpu.VMEM((SUBLANES - 1, ts + B_HDR - SUBLANES, W_MIX), F32),
            pltpu.VMEM((ts, W_MIX), BF16),
            pltpu.VMEM((ts, W_MIX), F32),
            pltpu.VMEM((ts, N_HEADS * BLK), F32),
            pltpu.VMEM((ts, 3 * W_MIX), BF16),
        ],
        compiler_params=pltpu.CompilerParams(
            dimension_semantics=("arbitrary", "arbitrary"),
            vmem_limit_bytes=VMEM_LIMIT_BYTES),
        name="mixer",
    )(*args)


def _ffn_call(x, g2, w_up, cfw, w_dn, gf, *, tm, fc, final_norm):
    bsz, s_len, d = x.shape
    d_ff = w_dn.shape[0]
    tile = pl.BlockSpec((1, tm, d), lambda b, j: (b, j, 0))
    args = (x, g2, w_up, cfw, w_dn, gf)
    in_specs = [tile] + [_full(a.shape) for a in args[1:]]
    return pl.pallas_call(
        functools.partial(_ffn_kernel, tm=tm, d_ff=d_ff, fc=fc, final_norm=final_norm),
        grid=(bsz, s_len // tm),
        in_specs=in_specs,
        out_specs=tile,
        out_shape=jax.ShapeDtypeStruct(x.shape, x.dtype),
        scratch_shapes=[
            pltpu.VMEM((d_ff // fc, SUBLANES + tm, fc), F32),
            pltpu.VMEM((d_ff // fc, SUBLANES + tm, fc), F32),
        ],
        compiler_params=pltpu.CompilerParams(
            dimension_semantics=("arbitrary", "arbitrary"),
            vmem_limit_bytes=VMEM_LIMIT_BYTES),
        name="ffn",
    )(*args)


def kernel(x, norm1_g, w_in, conv_a_w, conv_b_w, conv_b_b, ln_b_g, ln_b_b, ln_c_g, ln_c_b,
           sgu_w, sgu_b, w_out, norm2_g, w_up, conv_f_w, w_down, final_g):
    depth = w_in.shape[0]
    row = lambda v: v[None, :]
    for l in range(depth):
        sgub = jnp.repeat(sgu_b[l].T, HEAD_DIM, axis=1)
        x = _mixer_call(
            x, row(norm1_g[l]), w_in[l].astype(BF16), conv_a_w[l], conv_b_w[l], row(conv_b_b[l]),
            row(ln_b_g[l]), row(ln_b_b[l]), row(ln_c_g[l]), row(ln_c_b[l]), sgu_w[l], sgub,
            w_out[l].astype(BF16), ts=512)
        x = _ffn_call(
            x, row(norm2_g[l]), w_up[l].astype(BF16), conv_f_w[l], w_down[l].astype(BF16),
            row(final_g), tm=512, fc=256, final_norm=(l == depth - 1))
    return x
```

```python
import functools

import jax
import jax.numpy as jnp
from jax import lax
from jax.experimental import pallas as pl
from jax.experimental.pallas import tpu as pltpu

F32 = jnp.float32
BF16 = jnp.bfloat16

HEAD_DIM = 64
N_HEADS = 4
W_MIX = N_HEADS * HEAD_DIM
BLK = 128
K_SHORT = 3
K_CONF = 31
K_FFN = 3
RMS_EPS = 1e-6
LN_EPS = 1e-5
SUBLANES = 8
LOG2_E = 1.4426950408889634
A_HDR = SUBLANES
B_HDR = 4 * SUBLANES
CONV_ROWS = 64
VMEM_LIMIT_BYTES = 56 * 1024 * 1024
EXP_FLOOR = -110.0
OLDER_UNROLL = 2


def _dot(a, b):
    return jnp.dot(a, b, preferred_element_type=F32)


def _dot_nt(a, b):
    return lax.dot_general(a, b, (((1,), (1,)), ((), ())), preferred_element_type=F32)


def _rmsnorm(x, g):
    return x * lax.rsqrt(jnp.mean(x * x, axis=-1, keepdims=True) + RMS_EPS) * g


def _layernorm(x, g, b):
    mu = jnp.mean(x, axis=-1, keepdims=True)
    xc = x - mu
    var = jnp.mean(xc * xc, axis=-1, keepdims=True)
    return xc * lax.rsqrt(var + LN_EPS) * g + b


def _silu(x):
    return x * jax.nn.sigmoid(x)


def _split_bf16(x):
    hi = x.astype(BF16)
    lo = (x - hi.astype(F32)).astype(BF16)
    return hi, lo


def _mixer_kernel(x_ref, g1_ref, win_ref, caw_ref, cbw_ref, cbb_ref, lnbg_ref, lnbb_ref,
                  lncg_ref, lncb_ref, sguw_ref, sgub_ref, wout_ref, o_ref,
                  kbd_ref, vbd_ref, abuf, bbuf, sbuf, q_ref, oacc, crun, mix_ref, *, ts):
    j = pl.program_id(1)
    nblk = ts // BLK

    @pl.when(j == 0)
    def _():
        abuf[0:A_HDR, :] = jnp.zeros((A_HDR, W_MIX), F32)
        bbuf[0:B_HDR, :] = jnp.zeros((B_HDR, W_MIX), F32)

    x = x_ref[0]
    h = _rmsnorm(x, g1_ref[...]).astype(BF16)

    lane_head = lax.broadcasted_iota(jnp.int32, (BLK, W_MIX), 1) // HEAD_DIM

    def head_stack(blk):
        parts = [jnp.where(lane_head == g, blk, 0.0) for g in range(N_HEADS)]
        return jnp.concatenate(parts, axis=0).astype(BF16)

    pb = _dot(h, win_ref[:, 3 * W_MIX:5 * W_MIX])
    bbuf[B_HDR:B_HDR + ts, :] = pb[:, 0:W_MIX] * jax.nn.sigmoid(pb[:, W_MIX:2 * W_MIX])
    n_shift_rows = ts + B_HDR - SUBLANES
    for s in range(1, SUBLANES):
        sbuf[s - 1, :, :] = bbuf[s:s + n_shift_rows, :]

    def conformer_rows(r):
        acc = jnp.zeros((CONV_ROWS, W_MIX), F32) + cbb_ref[...]
        for k in range(K_CONF):
            grp, s = divmod(B_HDR - (K_CONF - 1) + k, SUBLANES)
            off = r * CONV_ROWS + grp * SUBLANES
            src = bbuf[off:off + CONV_ROWS, :] if s == 0 else sbuf[s - 1, off:off + CONV_ROWS, :]
            acc = acc + cbw_ref[k:k + 1, :] * src
        yb = _silu(_layernorm(acc, lnbg_ref[...], lnbb_ref[...]))
        mix_ref[r * CONV_ROWS:(r + 1) * CONV_ROWS, W_MIX:2 * W_MIX] = yb.astype(BF16)

    pd = _dot(h, win_ref[:, 7 * W_MIX:10 * W_MIX])
    q_ref[...] = (pd[:, 0:W_MIX] * (HEAD_DIM ** -0.5)).astype(BF16)
    for n in range(nblk):
        rows = slice(n * BLK, (n + 1) * BLK)
        kbd_ref[j * nblk + n] = head_stack(pd[rows, W_MIX:2 * W_MIX])
        vbd_ref[j * nblk + n] = head_stack(pd[rows, 2 * W_MIX:3 * W_MIX])

    r2 = lax.broadcasted_iota(jnp.int32, (2 * BLK, 2 * BLK), 0)
    c2 = lax.broadcasted_iota(jnp.int32, (2 * BLK, 2 * BLK), 1)
    ubd = jnp.where((r2 > c2) & ((r2 < BLK) | (c2 >= BLK)), 1.0, 0.0).astype(BF16)

    t_loc = lax.broadcasted_iota(jnp.int32, (BLK, N_HEADS * BLK), 0)
    s_loc = lax.broadcasted_iota(jnp.int32, (BLK, N_HEADS * BLK), 1) % BLK
    causal = s_loc < t_loc

    def mask_own_block(v, fill):
        own = jnp.where(causal, v[0:BLK], fill)
        return own if v.shape[0] == BLK else jnp.concatenate([own, v[BLK:]], axis=0)

    def attend(q, kblk, vblk, carry, diag):
        z = _dot_nt(q, kblk)
        sp = jnp.log(1.0 + jnp.exp2(jnp.abs(z) * -LOG2_E))
        log_beta = jnp.minimum(z, 0.0) - sp
        lom = log_beta - z
        if diag:
            lom = mask_own_block(lom, 0.0)
        hi, lo = _split_bf16(lom)
        after = []
        for p in range(N_HEADS // 2):
            cols = slice(p * 2 * BLK, (p + 1) * 2 * BLK)
            after.append(_dot(hi[:, cols], ubd) + _dot(lo[:, cols], ubd))
        tot = []
        for g in range(N_HEADS):
            row_sum = jnp.sum(lom[:, g * BLK:(g + 1) * BLK], axis=-1, keepdims=True)
            tot.append(jnp.broadcast_to(row_sum, (lom.shape[0], BLK)))
        logit = log_beta + jnp.concatenate(after, axis=1) + carry
        if diag:
            logit = mask_own_block(logit, -jnp.inf)
        a = jnp.exp(logit).astype(BF16)
        return _dot(a, vblk), jnp.concatenate(tot, axis=1)

    oacc[...] = jnp.zeros((ts, W_MIX), F32)
    crun[...] = jnp.zeros((ts, N_HEADS * BLK), F32)

    def own_tile_stage(m):
        rows = slice(m * BLK, ts)
        do, tot = attend(q_ref[rows, :], kbd_ref[j * nblk + m], vbd_ref[j * nblk + m],
                         crun[rows, :], True)
        oacc[rows, :] += do
        crun[rows, :] += tot

    conv_chunks = ts // CONV_ROWS
    for r in range(0, conv_chunks // 4):
        conformer_rows(r)

    pa = _dot(h, win_ref[:, 0:3 * W_MIX])
    abuf[A_HDR:A_HDR + ts, :] = pa[:, W_MIX:2 * W_MIX] * pa[:, 2 * W_MIX:3 * W_MIX]
    conv_a = jnp.zeros((ts, W_MIX), F32)
    for k in range(K_SHORT):
        off = A_HDR - (K_SHORT - 1) + k
        conv_a = conv_a + caw_ref[k:k + 1, :] * abuf[off:off + ts, :]
    mix_ref[:, 0:W_MIX] = (pa[:, 0:W_MIX] * conv_a).astype(BF16)
    abuf[0:A_HDR, :] = abuf[ts:ts + A_HDR, :]
    out = x + _dot(mix_ref[:, 0:W_MIX], wout_ref[0:W_MIX, :])
    for r in range(conv_chunks // 4, 5 * conv_chunks // 8):
        conformer_rows(r)

    pc = _dot(h, win_ref[:, 5 * W_MIX:7 * W_MIX])
    uv = 0.5 * pc * (1.0 + lax.erf(pc * (2.0 ** -0.5)))
    u_c = uv[:, 0:W_MIX]
    v_c = _layernorm(uv[:, W_MIX:2 * W_MIX], lncg_ref[...], lncb_ref[...])
    row = lax.broadcasted_iota(jnp.int32, (BLK, BLK), 0)
    col = lax.broadcasted_iota(jnp.int32, (BLK, BLK), 1)
    ws_cat = jnp.concatenate(
        [jnp.where(col <= row, sguw_ref[g], 0.0) for g in range(N_HEADS)], axis=1).astype(BF16)
    for n in range(nblk):
        rows = slice(n * BLK, (n + 1) * BLK)
        sp = _dot(ws_cat, head_stack(v_c[rows, :])) + sgub_ref[...]
        mix_ref[rows, 2 * W_MIX:3 * W_MIX] = (u_c[rows, :] * sp).astype(BF16)
    out = out + _dot(mix_ref[:, 2 * W_MIX:3 * W_MIX], wout_ref[2 * W_MIX:3 * W_MIX, :])
    for r in range(5 * conv_chunks // 8, conv_chunks):
        conformer_rows(r)
    bbuf[0:B_HDR, :] = bbuf[ts:ts + B_HDR, :]
    out = out + _dot(mix_ref[:, W_MIX:2 * W_MIX], wout_ref[W_MIX:2 * W_MIX, :])

    for m in reversed(range(nblk)):
        own_tile_stage(m)
    o_ref[0] = out

    n_older = j * nblk

    def older_tiles(rows):
        def any_weight_left():
            top = jnp.max(jnp.max(crun[rows, :], axis=0, keepdims=True), axis=1, keepdims=True)
            return (top[0, 0] > EXP_FLOOR).astype(jnp.int32)

        def older_blocks(state):
            i, _ = state
            for u in range(OLDER_UNROLL):
                kb = n_older - 1 - i - u
                do, tot = attend(q_ref[rows, :], kbd_ref[kb], vbd_ref[kb], crun[rows, :], False)
                oacc[rows, :] += do
                crun[rows, :] += tot
            return i + OLDER_UNROLL, any_weight_left()

        lax.while_loop(lambda s: (s[0] < n_older) & (s[1] > 0), older_blocks,
                       (jnp.int32(0), any_weight_left()))

    for rows in (slice(0, ts // 2), slice(ts // 2, ts)):
        older_tiles(rows)
    o_ref[0] += _dot(oacc[...].astype(BF16), wout_ref[3 * W_MIX:4 * W_MIX, :])


def _ffn_kernel(x_ref, g2_ref, wup_ref, cfw_ref, wdn_ref, gf_ref, o_ref, gbuf, vbuf,
                *, tm, d_ff, fc, final_norm):
    j = pl.program_id(1)
    nc = d_ff // fc
    hdr = SUBLANES

    @pl.when(j == 0)
    def _():
        gbuf[:, 0:hdr, :] = jnp.zeros((nc, hdr, fc), F32)
        vbuf[:, 0:hdr, :] = jnp.zeros((nc, hdr, fc), F32)

    row_halves = (slice(0, tm // 2), slice(tm // 2, tm))
    whole_tile = (slice(0, tm),)
    x = x_ref[0]
    h2_halves = [_rmsnorm(x[rows], g2_ref[...]).astype(BF16) for rows in row_halves]
    h2 = jnp.concatenate(h2_halves, axis=0)

    def up_proj(c, rows):
        lhs = h2 if rows.stop - rows.start == tm else h2_halves[row_halves.index(rows)]
        for buf, base in ((gbuf, 0), (vbuf, d_ff)):
            cols = slice(base + c * fc, base + (c + 1) * fc)
            buf[c, hdr + rows.start:hdr + rows.stop, :] = _dot(lhs, wup_ref[:, cols])

    sub_idx = lax.broadcasted_iota(jnp.int32, (hdr, fc), 0)

    def conv_gate(c, rows):
        halves = []
        for buf, base in ((gbuf, 0), (vbuf, d_ff)):
            cols = slice(base + c * fc, base + (c + 1) * fc)
            u = buf[c, hdr + rows.start:hdr + rows.stop, :]
            prev = buf[c, rows.start:rows.start + hdr, :]
            conv = cfw_ref[K_FFN - 1:K_FFN, cols] * u
            for delay in range(1, K_FFN):
                sh = pltpu.roll(u, delay, axis=0)
                first = jnp.where(sub_idx < delay, pltpu.roll(prev, delay, axis=0), sh[0:hdr])
                sh = jnp.concatenate([first, sh[hdr:]], axis=0)
                conv = conv + cfw_ref[K_FFN - 1 - delay:K_FFN - delay, cols] * sh
            if rows.stop == tm:
                buf[c, 0:hdr, :] = u[rows.stop - rows.start - hdr:]
            halves.append(conv)
        return (_silu(halves[0]) * halves[1]).astype(BF16)

    def split(c):
        return row_halves if c in (0, nc - 1) else whole_tile

    acc = x
    acts = {}
    for i in range(nc + 2):
        if i < nc:
            for rows in (row_halves if i == 0 else whole_tile):
                up_proj(i, rows)
        if 0 <= i - 1 < nc:
            acts[i - 1] = [conv_gate(i - 1, rows) for rows in split(i - 1)]
        if 0 <= i - 2 < nc:
            c = i - 2
            parts = acts.pop(c)
            if c == nc - 1:
                for rows, act in zip(row_halves, parts):
                    out = acc[rows] + _dot(act, wdn_ref[c * fc:(c + 1) * fc, :])
                    if final_norm:
                        out = _rmsnorm(out, gf_ref[...])
                    o_ref[0, rows, :] = out
            else:
                act = parts[0] if len(parts) == 1 else jnp.concatenate(parts, axis=0)
                acc = acc + _dot(act, wdn_ref[c * fc:(c + 1) * fc, :])


def _full(shape):
    return pl.BlockSpec(shape, lambda b, j: (0,) * len(shape), pipeline_mode=pl.Buffered(1))


def _mixer_call(x, g1, w_in, caw, cbw, cbb, lnbg, lnbb, lncg, lncb, sguw, sgub, w_out, *, ts):
    bsz, s_len, d = x.shape
    tile = pl.BlockSpec((1, ts, d), lambda b, j: (b, j, 0))
    args = (x, g1, w_in, caw, cbw, cbb, lnbg, lnbb, lncg, lncb, sguw, sgub, w_out)
    in_specs = [tile] + [_full(a.shape) for a in args[1:]]
    return pl.pallas_call(
        functools.partial(_mixer_kernel, ts=ts),
        grid=(bsz, s_len // ts),
        in_specs=in_specs,
        out_specs=tile,
        out_shape=jax.ShapeDtypeStruct(x.shape, x.dtype),
        scratch_shapes=[
            pltpu.VMEM((s_len // BLK, N_HEADS * BLK, W_MIX), BF16),
            pltpu.VMEM((s_len // BLK, N_HEADS * BLK, W_MIX), BF16),
            pltpu.VMEM((A_HDR + ts, W_MIX), F32),
            pltpu.VMEM((B_HDR + ts, W_MIX), F32),
            pltpu.VMEM((SUBLANES - 1, ts + B_HDR - SUBLANES, W_MIX), F32),
            pltpu.VMEM((ts, W_MIX), BF16),
            pltpu.VMEM((ts, W_MIX), F32),
            pltpu.VMEM((ts, N_HEADS * BLK), F32),
            pltpu.VMEM((ts, 3 * W_MIX), BF16),
        ],
        compiler_params=pltpu.CompilerParams(
            dimension_semantics=("arbitrary", "arbitrary"),
            vmem_limit_bytes=VMEM_LIMIT_BYTES),
        name="mixer",
    )(*args)


def _ffn_call(x, g2, w_up, cfw, w_dn, gf, *, tm, fc, final_norm):
    bsz, s_len, d = x.shape
    d_ff = w_dn.shape[0]
    tile = pl.BlockSpec((1, tm, d), lambda b, j: (b, j, 0))
    args = (x, g2, w_up, cfw, w_dn, gf)
    in_specs = [tile] + [_full(a.shape) for a in args[1:]]
    return pl.pallas_call(
        functools.partial(_ffn_kernel, tm=tm, d_ff=d_ff, fc=fc, final_norm=final_norm),
        grid=(bsz, s_len // tm),
        in_specs=in_specs,
        out_specs=tile,
        out_shape=jax.ShapeDtypeStruct(x.shape, x.dtype),
        scratch_shapes=[
            pltpu.VMEM((d_ff // fc, SUBLANES + tm, fc), F32),
            pltpu.VMEM((d_ff // fc, SUBLANES + tm, fc), F32),
        ],
        compiler_params=pltpu.CompilerParams(
            dimension_semantics=("arbitrary", "arbitrary"),
            vmem_limit_bytes=VMEM_LIMIT_BYTES),
        name="ffn",
    )(*args)


def kernel(x, norm1_g, w_in, conv_a_w, conv_b_w, conv_b_b, ln_b_g, ln_b_b, ln_c_g, ln_c_b,
           sgu_w, sgu_b, w_out, norm2_g, w_up, conv_f_w, w_down, final_g):
    depth = w_in.shape[0]
    row = lambda v: v[None, :]
    for l in range(depth):
        sgub = jnp.repeat(sgu_b[l].T, HEAD_DIM, axis=1)
        x = _mixer_call(
            x, row(norm1_g[l]), w_in[l].astype(BF16), conv_a_w[l], conv_b_w[l], row(conv_b_b[l]),
            row(ln_b_g[l]), row(ln_b_b[l]), row(ln_c_g[l]), row(ln_c_b[l]), sgu_w[l], sgub,
            w_out[l].astype(BF16), ts=512)
        x = _ffn_call(
            x, row(norm2_g[l]), w_up[l].astype(BF16), conv_f_w[l], w_down[l].astype(BF16),
            row(final_g), tm=512, fc=256, final_norm=(l == depth - 1))
    return x
```
